```python
import jax, jax.numpy as jnp
from jax import lax
import numpy as np

D_MODEL = 4096
BATCH = 4
SEQ = 4096
DEPTH = 4

GRID_W = 64
CTX_LEN = 256
N_MIXERS = 4
HEAD_DIM = 128
N_HEADS = D_MODEL // HEAD_DIM
N_KV_HEADS = N_HEADS // 4
Q_PER_KV = N_HEADS // N_KV_HEADS
QKV_DIM = (N_HEADS + 2 * N_KV_HEADS) * HEAD_DIM
Q_BLOCK = 128
WINDOW = 128
NA_KH = 8
NA_KW = 16
MLA_Q_RANK = D_MODEL // 4
MLA_KV_RANK = D_MODEL // 8
MLA_NOPE = 128
MLA_ROPE = 64
MLA_V = 128
D_FF = 3 * D_MODEL // 2
CONV_W = 3
ADA_CHUNKS = 6
ROPE_THETA = 10000.0
EPS = 1e-6
NEG_INF = -1e30

kernel_name = "hybrid_interleaved_diffusion_trunk"


def rms_norm(x, g):
    xf = x.astype(jnp.float32)
    y = xf * lax.rsqrt(jnp.mean(xf * xf, axis=-1, keepdims=True) + EPS)
    return (y * g.astype(jnp.float32)).astype(x.dtype)


def modulate(x, g, shift, scale):
    return rms_norm(x, g) * (1 + scale) + shift


def axial_rope_tables(n_tok, rot_dim):
    n_freq = rot_dim // 4
    freqs = ROPE_THETA ** (-jnp.arange(n_freq, dtype=jnp.float32) / n_freq)
    t = jnp.arange(n_tok)
    row = (t // GRID_W).astype(jnp.float32)
    col = (t % GRID_W).astype(jnp.float32)
    ang = jnp.concatenate([row[:, None] * freqs, col[:, None] * freqs], axis=-1)
    return jnp.cos(ang), jnp.sin(ang)


def apply_rope(x, cos, sin):
    half = x.shape[-1] // 2
    shape = (cos.shape[0],) + (1,) * (x.ndim - 3) + (half,)
    cos = cos.reshape(shape)
    sin = sin.reshape(shape)
    xf = x.astype(jnp.float32)
    x1, x2 = xf[..., :half], xf[..., half:]
    return jnp.concatenate([x1 * cos - x2 * sin, x1 * sin + x2 * cos], axis=-1).astype(x.dtype)


def gqa_project(h, w_qkv):
    b, n, _ = h.shape
    q, k, v = jnp.split(h @ w_qkv, [N_HEADS * HEAD_DIM, (N_HEADS + N_KV_HEADS) * HEAD_DIM], axis=-1)
    return (q.reshape(b, n, N_KV_HEADS, Q_PER_KV, HEAD_DIM),
            k.reshape(b, n, N_KV_HEADS, HEAD_DIM),
            v.reshape(b, n, N_KV_HEADS, HEAD_DIM))


def attend(q, k, v, scale, mask=None, sink=None):
    s = jnp.einsum('bqgrd,bkgd->bgrqk', q, k).astype(jnp.float32) * scale
    if mask is not None:
        s = jnp.where(mask, s, NEG_INF)
    if sink is not None:
        sk = jnp.broadcast_to(sink.astype(jnp.float32)[None, :, :, None, None], s.shape[:-1] + (1,))
        p = jax.nn.softmax(jnp.concatenate([s, sk], axis=-1), axis=-1)[..., :-1]
    else:
        p = jax.nn.softmax(s, axis=-1)
    return jnp.einsum('bgrqk,bkgd->bqgrd', p.astype(v.dtype), v)


def to_blocks(a):
    nb = a.shape[1] // Q_BLOCK
    return a.reshape((a.shape[0], nb, Q_BLOCK) + a.shape[2:]).swapaxes(0, 1)


def from_blocks(o):
    return o.swapaxes(0, 1).reshape((o.shape[1], o.shape[0] * o.shape[2]) + o.shape[3:])


def dense_blocks(q, k, v, scale):
    return from_blocks(lax.map(lambda qb: attend(qb, k, v, scale), to_blocks(q)))


def merge_heads(o, w_o):
    return o.reshape(o.shape[0], o.shape[1], -1) @ w_o


def mixer_global(hx, hc, need_ctx, w_qkv, q_norm_g, k_norm_g, w_o):
    n_lat = hx.shape[1]
    qx, kx, vx = gqa_project(hx, w_qkv)
    qc, kc, vc = gqa_project(hc, w_qkv)
    cos, sin = axial_rope_tables(n_lat, HEAD_DIM)
    qx = apply_rope(rms_norm(qx, q_norm_g), cos, sin)
    kx = apply_rope(rms_norm(kx, k_norm_g), cos, sin)
    kc = rms_norm(kc, k_norm_g)
    scale = HEAD_DIM ** -0.5
    ox = dense_blocks(qx, jnp.concatenate([kc, kx], axis=1), jnp.concatenate([vc, vx], axis=1), scale)
    out_x = merge_heads(ox, w_o)
    out_c = merge_heads(attend(rms_norm(qc, q_norm_g), kc, vc, scale), w_o) if need_ctx else None
    return out_x, out_c


def mixer_window(hx, hc, need_ctx, w_qkv, sink, w_o):
    n_lat, n_ctx = hx.shape[1], hc.shape[1]
    qx, kx, vx = gqa_project(hx, w_qkv)
    qc, kc, vc = gqa_project(hc, w_qkv)
    cos, sin = axial_rope_tables(n_lat, HEAD_DIM)
    qx = apply_rope(qx, cos, sin)
    kx = apply_rope(kx, cos, sin)
    sink_gr = sink.reshape(N_KV_HEADS, Q_PER_KV)
    scale = HEAD_DIM ** -0.5
    span = Q_BLOCK + 2 * WINDOW
    pad = ((0, 0), (WINDOW, WINDOW), (0, 0), (0, 0))
    kpad = jnp.pad(kx, pad)
    vpad = jnp.pad(vx, pad)
    ctx_mask = jnp.ones((Q_BLOCK, n_ctx), dtype=bool)

    def block(args):
        qb, blk = args
        start = blk * Q_BLOCK
        kb = lax.dynamic_slice_in_dim(kpad, start, span, axis=1)
        vb = lax.dynamic_slice_in_dim(vpad, start, span, axis=1)
        qpos = start + jnp.arange(Q_BLOCK)
        kpos = start - WINDOW + jnp.arange(span)
        band = ((jnp.abs(qpos[:, None] - kpos[None, :]) <= WINDOW)
                & (kpos >= 0)[None, :] & (kpos < n_lat)[None, :])
        return attend(qb, jnp.concatenate([kc, kb], axis=1), jnp.concatenate([vc, vb], axis=1), scale,
                      mask=jnp.concatenate([ctx_mask, band], axis=1), sink=sink_gr)

    nb = n_lat // Q_BLOCK
    out_x = merge_heads(from_blocks(lax.map(block, (to_blocks(qx), jnp.arange(nb)))), w_o)
    out_c = merge_heads(attend(qc, kc, vc, scale, sink=sink_gr), w_o) if need_ctx else None
    return out_x, out_c


def mixer_neighbourhood(hx, hc, need_ctx, w_qkv, rel_bias, w_o):
    n_lat, n_ctx = hx.shape[1], hc.shape[1]
    rows = n_lat // GRID_W
    kh = min(NA_KH, rows)
    nk = kh * NA_KW
    nb = n_lat // Q_BLOCK
    qx, kx, vx = gqa_project(hx, w_qkv)
    qc, kc, vc = gqa_project(hc, w_qkv)
    scale = HEAD_DIM ** -0.5
    t = jnp.arange(n_lat)
    r, col = t // GRID_W, t % GRID_W
    rs = jnp.clip(r - kh // 2, 0, rows - kh)
    cs = jnp.clip(col - NA_KW // 2, 0, GRID_W - NA_KW)
    kr = rs[:, None] + jnp.arange(kh)
    kcol = cs[:, None] + jnp.arange(NA_KW)
    nbr = (kr[:, :, None] * GRID_W + kcol[:, None, :]).reshape(n_lat, nk)
    dr = kr - r[:, None] + (NA_KH - 1)
    dc = kcol - col[:, None] + (NA_KW - 1)
    bias = rel_bias[:, dr[:, :, None], dc[:, None, :]].astype(jnp.float32)
    bias = bias.reshape(N_KV_HEADS, Q_PER_KV, nb, Q_BLOCK, nk).transpose(2, 0, 1, 3, 4)
    idx = nbr.reshape(nb, Q_BLOCK, nk)

    def block(args):
        qb, ib, bb = args
        kg = kx[:, ib]
        vg = vx[:, ib]
        s_loc = jnp.einsum('bqgrd,bqkgd->bgrqk', qb, kg).astype(jnp.float32) * scale + bb
        s_ctx = jnp.einsum('bqgrd,bkgd->bgrqk', qb, kc).astype(jnp.float32) * scale
        p = jax.nn.softmax(jnp.concatenate([s_ctx, s_loc], axis=-1), axis=-1).astype(vx.dtype)
        return (jnp.einsum('bgrqk,bkgd->bqgrd', p[..., :n_ctx], vc)
                + jnp.einsum('bgrqk,bqkgd->bqgrd', p[..., n_ctx:], vg))

    out_x = merge_heads(from_blocks(lax.map(block, (to_blocks(qx), idx, bias))), w_o)
    out_c = merge_heads(attend(qc, kc, vc, scale), w_o) if need_ctx else None
    return out_x, out_c


def mixer_mla(hx, hc, need_ctx, w_dq, q_norm_g, w_uq, w_dkv, kv_norm_g, w_ukv, w_o):
    def project(h, rope):
        b, n, _ = h.shape
        cq = rms_norm(h @ w_dq, q_norm_g)
        q = (cq @ w_uq).reshape(b, n, N_HEADS, MLA_NOPE + MLA_ROPE)
        kv_a = h @ w_dkv
        ckv = rms_norm(kv_a[..., :MLA_KV_RANK], kv_norm_g)
        k_pe = kv_a[..., MLA_KV_RANK:]
        kv = (ckv @ w_ukv).reshape(b, n, N_HEADS, MLA_NOPE + MLA_V)
        k_nope, v = kv[..., :MLA_NOPE], kv[..., MLA_NOPE:]
        q_nope, q_pe = q[..., :MLA_NOPE], q[..., MLA_NOPE:]
        if rope is not None:
            cos, sin = rope
            q_pe = apply_rope(q_pe, cos, sin)
            k_pe = apply_rope(k_pe, cos, sin)
        k = jnp.concatenate([k_nope, jnp.broadcast_to(k_pe[:, :, None, :], (b, n, N_HEADS, MLA_ROPE))], axis=-1)
        q = jnp.concatenate([q_nope, q_pe], axis=-1)[:, :, :, None, :]
        return q, k, v

    n_lat = hx.shape[1]
    qx, kx, vx = project(hx, axial_rope_tables(n_lat, MLA_ROPE))
    qc, kc, vc = project(hc, None)
    scale = (MLA_NOPE + MLA_ROPE) ** -0.5
    ox = dense_blocks(qx, jnp.concatenate([kc, kx], axis=1), jnp.concatenate([vc, vx], axis=1), scale)
    out_x = merge_heads(ox, w_o)
    out_c = merge_heads(attend(qc, kc, vc, scale), w_o) if need_ctx else None
    return out_x, out_c


def conv_ffn(h, w_up, conv_w, conv_b, w_down):
    a, v = jnp.split(h @ w_up, 2, axis=-1)
    a = lax.conv_general_dilated(a, conv_w[:, None, :].astype(a.dtype), (1,),
                                 [(CONV_W // 2, CONV_W // 2)],
                                 dimension_numbers=('NWC', 'WIO', 'NWC'),
                                 feature_group_count=a.shape[-1]) + conv_b
    return (jax.nn.silu(a) * v) @ w_down


def setup_inputs(seed: int = 0) -> dict:
    key = jax.random.key(seed)
    keys = iter(jax.random.split(key, 128))

    def nrm(shape, scale):
        return jax.random.normal(next(keys), shape, jnp.float32) * scale

    def gain(n):
        return 1.0 + nrm((n,), 0.02)

    d = D_MODEL
    inp = {
        'x': nrm((BATCH, SEQ, d), 1.0),
        'c': nrm((BATCH, d), 1.0),
        'ctx': nrm((BATCH, CTX_LEN, d), 1.0),
        'c_ctx': nrm((d,), 1.0),
    }
    for i in range(DEPTH):
        p = 'l%d_' % i
        kind = i % N_MIXERS
        inp[p + 'ada_w'] = nrm((d, ADA_CHUNKS * d), 0.5 * d ** -0.5)
        inp[p + 'ada_b'] = nrm((ADA_CHUNKS * d,), 0.01)
        inp[p + 'attn_norm_g'] = gain(d)
        if kind < 3:
            inp[p + 'w_qkv'] = nrm((d, QKV_DIM), d ** -0.5)
        if kind == 0:
            inp[p + 'q_norm_g'] = gain(HEAD_DIM)
            inp[p + 'k_norm_g'] = gain(HEAD_DIM)
        elif kind == 1:
            inp[p + 'sink'] = nrm((N_HEADS,), 0.5)
        elif kind == 2:
            inp[p + 'rel_bias'] = nrm((N_HEADS, 2 * NA_KH - 1, 2 * NA_KW - 1), 0.1)
        else:
            inp[p + 'w_dq'] = nrm((d, MLA_Q_RANK), d ** -0.5)
            inp[p + 'q_norm_g'] = gain(MLA_Q_RANK)
            inp[p + 'w_uq'] = nrm((MLA_Q_RANK, N_HEADS * (MLA_NOPE + MLA_ROPE)), MLA_Q_RANK ** -0.5)
            inp[p + 'w_dkv'] = nrm((d, MLA_KV_RANK + MLA_ROPE), d ** -0.5)
            inp[p + 'kv_norm_g'] = gain(MLA_KV_RANK)
            inp[p + 'w_ukv'] = nrm((MLA_KV_RANK, N_HEADS * (MLA_NOPE + MLA_V)), MLA_KV_RANK ** -0.5)
        o_in = N_HEADS * (MLA_V if kind == 3 else HEAD_DIM)
        inp[p + 'w_o'] = nrm((o_in, d), o_in ** -0.5)
        inp[p + 'ffn_norm_g'] = gain(d)
        inp[p + 'ffn_w_up'] = nrm((d, 2 * D_FF), d ** -0.5)
        inp[p + 'ffn_conv_w'] = nrm((CONV_W, D_FF), CONV_W ** -0.5)
        inp[p + 'ffn_conv_b'] = nrm((D_FF,), 0.01)
        inp[p + 'ffn_w_down'] = nrm((D_FF, d), D_FF ** -0.5)
    inp['final_norm_g'] = gain(d)
    return inp


def reference(x, c, ctx, c_ctx,
              l0_ada_w, l0_ada_b, l0_attn_norm_g, l0_w_qkv, l0_q_norm_g, l0_k_norm_g, l0_w_o,
              l0_ffn_norm_g, l0_ffn_w_up, l0_ffn_conv_w, l0_ffn_conv_b, l0_ffn_w_down,
              l1_ada_w, l1_ada_b, l1_attn_norm_g, l1_w_qkv, l1_sink, l1_w_o,
              l1_ffn_norm_g, l1_ffn_w_up, l1_ffn_conv_w, l1_ffn_conv_b, l1_ffn_w_down,
              l2_ada_w, l2_ada_b, l2_attn_norm_g, l2_w_qkv, l2_rel_bias, l2_w_o,
              l2_ffn_norm_g, l2_ffn_w_up, l2_ffn_conv_w, l2_ffn_conv_b, l2_ffn_w_down,
              l3_ada_w, l3_ada_b, l3_attn_norm_g, l3_w_dq, l3_q_norm_g, l3_w_uq, l3_w_dkv,
              l3_kv_norm_g, l3_w_ukv, l3_w_o,
              l3_ffn_norm_g, l3_ffn_w_up, l3_ffn_conv_w, l3_ffn_conv_b, l3_ffn_w_down,
              final_norm_g):
    b, _, d = x.shape
    ada_w = [l0_ada_w, l1_ada_w, l2_ada_w, l3_ada_w]
    ada_b = [l0_ada_b, l1_ada_b, l2_ada_b, l3_ada_b]
    attn_g = [l0_attn_norm_g, l1_attn_norm_g, l2_attn_norm_g, l3_attn_norm_g]
    ffn_g = [l0_ffn_norm_g, l1_ffn_norm_g, l2_ffn_norm_g, l3_ffn_norm_g]
    ffn_p = [(l0_ffn_w_up, l0_ffn_conv_w, l0_ffn_conv_b, l0_ffn_w_down),
             (l1_ffn_w_up, l1_ffn_conv_w, l1_ffn_conv_b, l1_ffn_w_down),
             (l2_ffn_w_up, l2_ffn_conv_w, l2_ffn_conv_b, l2_ffn_w_down),
             (l3_ffn_w_up, l3_ffn_conv_w, l3_ffn_conv_b, l3_ffn_w_down)]
    mixer_p = [(l0_w_qkv, l0_q_norm_g, l0_k_norm_g, l0_w_o),
               (l1_w_qkv, l1_sink, l1_w_o),
               (l2_w_qkv, l2_rel_bias, l2_w_o),
               (l3_w_dq, l3_q_norm_g, l3_w_uq, l3_w_dkv, l3_kv_norm_g, l3_w_ukv, l3_w_o)]
    mixers = (mixer_global, mixer_window, mixer_neighbourhood, mixer_mla)

    cx = ctx
    sc = jax.nn.silu(c)
    sc_ctx = jax.nn.silu(c_ctx)
    for i in range(DEPTH):
        need_ctx = i < DEPTH - 1
        mod = (sc @ ada_w[i] + ada_b[i]).reshape(b, ADA_CHUNKS, 1, d)
        mc = (sc_ctx @ ada_w[i] + ada_b[i]).reshape(ADA_CHUNKS, d)
        hx = modulate(x, attn_g[i], mod[:, 0], mod[:, 1])
        hc = modulate(cx, attn_g[i], mc[0], mc[1])
        ax, ac = mixers[i % N_MIXERS](hx, hc, need_ctx, *mixer_p[i])
        x = x + mod[:, 2] * ax
        x = x + mod[:, 5] * conv_ffn(modulate(x, ffn_g[i], mod[:, 3], mod[:, 4]), *ffn_p[i])
        if need_ctx:
            cx = cx + mc[2] * ac
            cx = cx + mc[5] * conv_ffn(modulate(cx, ffn_g[i], mc[3], mc[4]), *ffn_p[i])
    return rms_norm(x, final_norm_g)
```

```python
import functools

import jax
import jax.numpy as jnp
import numpy as np
from jax import lax
from jax.experimental import pallas as pl
from jax.experimental.pallas import tpu as pltpu

F32 = jnp.float32
BF16 = jnp.bfloat16

GRID_W = 64
HEAD_DIM = 128
N_HEADS = 32
N_KV_HEADS = 8
Q_PER_KV = 4
WINDOW = 128
NA_KH = 8
NA_KW = 16
NA_KEY_ROWS = 12
MLA_NOPE = 128
MLA_ROPE = 64
MLA_V = 128
MLA_QK = 256
ROPE_THETA = 10000.0
EPS = 1e-6
NEG_INF = -1e30
LANES = 128
TQ = 256
MIB = 1024 * 1024


def _cparams(n_axes, vmem_mib):
    return pltpu.CompilerParams(dimension_semantics=("arbitrary",) * n_axes,
                                vmem_limit_bytes=vmem_mib * MIB)


def _dot(a, b):
    return jnp.dot(a, b, preferred_element_type=F32)


def _dot_nt(a, b):
    return lax.dot_general(a, b, (((1,), (1,)), ((), ())), preferred_element_type=F32)


def _rms(y, g):
    return y * lax.rsqrt(jnp.mean(y * y, axis=-1, keepdims=True) + EPS) * g


def _ada_kernel(c_ref, w_ref, b_ref, o_ref):
    x = c_ref[...]
    s = x / (1.0 + jnp.exp(-x))
    o_ref[...] = _dot(s.astype(BF16), w_ref[...].astype(BF16)) + b_ref[...]


def _ada(c8, w, b, tn=512):
    d, n = w.shape
    return pl.pallas_call(
        _ada_kernel,
        out_shape=jax.ShapeDtypeStruct((8, n), F32),
        grid=(n // tn,),
        in_specs=[pl.BlockSpec((8, d), lambda j: (0, 0)),
                  pl.BlockSpec((d, tn), lambda j: (0, j)),
                  pl.BlockSpec((1, tn), lambda j: (0, j))],
        out_specs=pl.BlockSpec((8, tn), lambda j: (0, j)),
        compiler_params=_cparams(1, 40),
    )(c8, w, b.reshape(1, n))


def _normmod_kernel(x_ref, g_ref, mod_ref, o_ref, *, shift_idx, scale_idx):
    y = _rms(x_ref[...], g_ref[...])
    y = y * (1.0 + mod_ref[0, scale_idx:scale_idx + 1, :]) + mod_ref[0, shift_idx:shift_idx + 1, :]
    o_ref[...] = y.astype(o_ref.dtype)


def _normmod(x, g, mod, seg, *, rows, shift_idx, scale_idx, tm=256):
    d = x.shape[1]
    return pl.pallas_call(
        functools.partial(_normmod_kernel, shift_idx=shift_idx, scale_idx=scale_idx),
        out_shape=jax.ShapeDtypeStruct((rows, d), BF16),
        grid=(rows // tm,),
        in_specs=[pl.BlockSpec((tm, d), lambda i: (i, 0)),
                  pl.BlockSpec((1, d), lambda i: (0, 0)),
                  pl.BlockSpec((1, 6, d), lambda i: (seg(i, tm), 0, 0))],
        out_specs=pl.BlockSpec((tm, d), lambda i: (i, 0)),
        compiler_params=_cparams(1, 32),
    )(x, g.reshape(1, d), mod)


def _final_norm_kernel(x_ref, g_ref, o_ref):
    o_ref[...] = _rms(x_ref[...], g_ref[...])


def _final_norm(x, g, *, rows, tm=256):
    d = x.shape[1]
    return pl.pallas_call(
        _final_norm_kernel,
        out_shape=jax.ShapeDtypeStruct((rows, d), F32),
        grid=(rows // tm,),
        in_specs=[pl.BlockSpec((tm, d), lambda i: (i, 0)),
                  pl.BlockSpec((1, d), lambda i: (0, 0))],
        out_specs=pl.BlockSpec((tm, d), lambda i: (i, 0)),
        compiler_params=_cparams(1, 32),
    )(x, g.reshape(1, d))


def _rope128(y, cos, sin):
    return y * cos + pltpu.roll(y, HEAD_DIM // 2, 1) * sin


def _rope64(y, cos, s1, s2):
    half = MLA_ROPE // 2
    return y * cos + pltpu.roll(y, half, 1) * s1 + pltpu.roll(y, LANES - half, 1) * s2


def _qkv_kernel(a_ref, w_ref, cos_ref, sin_ref, gq_ref, gk_ref, o_ref, *,
                nq_tiles, nqk_tiles, hpt, use_norm, use_rope, scale):
    j = pl.program_id(1)
    acc = _dot(a_ref[...], w_ref[...])

    @pl.when(j < nqk_tiles)
    def _():
        is_q = j < nq_tiles
        fac = jnp.where(is_q, scale, 1.0).astype(F32)
        g = jnp.where(is_q, gq_ref[...], gk_ref[...])
        for r in range(hpt):
            y = acc[:, r * HEAD_DIM:(r + 1) * HEAD_DIM]
            if use_norm:
                y = _rms(y, g)
            if use_rope:
                y = _rope128(y, cos_ref[...], sin_ref[...])
            o_ref[r] = (y * fac).astype(BF16)

    @pl.when(j >= nqk_tiles)
    def _():
        for r in range(hpt):
            o_ref[r] = acc[:, r * HEAD_DIM:(r + 1) * HEAD_DIM].astype(BF16)


def _qkv_proj(h, w, tabs, gq, gk, *, use_norm, use_rope, scale, tm=512, tn=512):
    t, d = h.shape
    n = w.shape[1]
    hpt = tn // HEAD_DIM
    cos, sin, tab_idx = tabs
    kern = functools.partial(
        _qkv_kernel, nq_tiles=N_HEADS * HEAD_DIM // tn, nqk_tiles=(N_HEADS + N_KV_HEADS) * HEAD_DIM // tn,
        hpt=hpt, use_norm=use_norm, use_rope=use_rope, scale=scale)
    return pl.pallas_call(
        kern,
        out_shape=jax.ShapeDtypeStruct((n // HEAD_DIM, t, HEAD_DIM), BF16),
        grid=(t // tm, n // tn),
        in_specs=[pl.BlockSpec((tm, d), lambda i, j: (i, 0)),
                  pl.BlockSpec((d, tn), lambda i, j: (0, j)),
                  pl.BlockSpec((tm, HEAD_DIM), lambda i, j: (tab_idx(i, tm), 0)),
                  pl.BlockSpec((tm, HEAD_DIM), lambda i, j: (tab_idx(i, tm), 0)),
                  pl.BlockSpec((1, HEAD_DIM), lambda i, j: (0, 0)),
                  pl.BlockSpec((1, HEAD_DIM), lambda i, j: (0, 0))],
        out_specs=pl.BlockSpec((hpt, tm, HEAD_DIM), lambda i, j: (j, i, 0)),
        compiler_params=_cparams(2, 40),
    )(h, w, cos, sin, gq.reshape(1, HEAD_DIM), gk.reshape(1, HEAD_DIM))


def _res_kernel(a_ref, w_ref, x_ref, mod_ref, o_ref, *, gate_idx):
    acc = _dot(a_ref[...], w_ref[...])
    o_ref[...] = x_ref[...] + mod_ref[0, gate_idx:gate_idx + 1, :] * acc


def _proj_res(a, w, xres, mod, seg, *, rows, gate_idx, tm=512, tn=512):
    k, n = w.shape
    return pl.pallas_call(
        functools.partial(_res_kernel, gate_idx=gate_idx),
        out_shape=jax.ShapeDtypeStruct((rows, n), F32),
        grid=(rows // tm, n // tn),
        in_specs=[pl.BlockSpec((tm, k), lambda i, j: (i, 0)),
                  pl.BlockSpec((k, tn), lambda i, j: (0, j)),
                  pl.BlockSpec((tm, tn), lambda i, j: (i, j)),
                  pl.BlockSpec((1, 6, tn), lambda i, j: (seg(i, tm), 0, j))],
        out_specs=pl.BlockSpec((tm, tn), lambda i, j: (i, j)),
        compiler_params=_cparams(2, 40),
    )(a, w, xres, mod)


def _plain_kernel(a_ref, w_ref, o_ref):
    o_ref[...] = _dot(a_ref[...], w_ref[...]).astype(o_ref.dtype)


def _proj_plain(a, w, *, rows, tm=512, tn=512):
    k, n = w.shape
    return pl.pallas_call(
        _plain_kernel,
        out_shape=jax.ShapeDtypeStruct((rows, n), BF16),
        grid=(rows // tm, n // tn),
        in_specs=[pl.BlockSpec((tm, k), lambda i, j: (i, 0)),
                  pl.BlockSpec((k, tn), lambda i, j: (0, j))],
        out_specs=pl.BlockSpec((tm, tn), lambda i, j: (i, j)),
        compiler_params=_cparams(2, 40),
    )(a, w)


def _ffn_down_kernel(a_ref, v_ref, ap_ref, an_ref, cw_ref, cb_ref, w_ref, x_ref, mod_ref, o_ref, g_scr, *,
                     tm, n_chunks, n_lat, n_ctx, lat_rows, gate_idx):
    i = pl.program_id(0)
    j = pl.program_id(1)

    @pl.when(j == 0)
    def _():
        row = lax.broadcasted_iota(jnp.int32, (tm, LANES), 0)
        grow = i * tm + row
        is_ctx = grow >= lat_rows
        first = ((grow & (n_ctx - 1)) == 0) & (is_ctx | ((grow & (n_lat - 1)) == 0))
        nxt = grow + 1
        last = ((nxt & (n_ctx - 1)) == 0) & (is_ctx | ((nxt & (n_lat - 1)) == 0))
        top = row == 0
        bot = row == tm - 1

        def chunk(c, carry):
            off = pl.multiple_of(c * LANES, LANES)
            sl = pl.ds(off, LANES)
            a = a_ref[:, sl].astype(F32)
            prev_row = ap_ref[7:8, sl].astype(F32)
            next_row = an_ref[0:1, sl].astype(F32)
            a_dn = jnp.where(top, prev_row, pltpu.roll(a, 1, 0))
            a_dn = jnp.where(first, 0.0, a_dn)
            a_up = jnp.where(bot, next_row, pltpu.roll(a, tm - 1, 0))
            a_up = jnp.where(last, 0.0, a_up)
            cv = cw_ref[0:1, sl] * a_dn + cw_ref[1:2, sl] * a + cw_ref[2:3, sl] * a_up + cb_ref[:, sl]
            gv = cv / (1.0 + jnp.exp(-cv)) * v_ref[:, sl].astype(F32)
            g_scr[:, sl] = gv.astype(BF16)
            return carry

        lax.fori_loop(0, n_chunks, chunk, 0)

    acc = _dot(g_scr[...], w_ref[...])
    o_ref[...] = x_ref[...] + mod_ref[0, gate_idx:gate_idx + 1, :] * acc


def _ffn_down(u, conv_w, conv_b, w, xres, mod, seg, *, rows, n_lat, n_ctx, lat_rows, gate_idx, tm=512, tn=512):
    dff, n = w.shape
    nb8 = u.shape[0] // 8
    kern = functools.partial(_ffn_down_kernel, tm=tm, n_chunks=dff // LANES, n_lat=n_lat, n_ctx=n_ctx,
                             lat_rows=lat_rows, gate_idx=gate_idx)
    return pl.pallas_call(
        kern,
        out_shape=jax.ShapeDtypeStruct((rows, n), F32),
        grid=(rows // tm, n // tn),
        in_specs=[pl.BlockSpec((tm, dff), lambda i, j: (i, 0)),
                  pl.BlockSpec((tm, dff), lambda i, j: (i, 1)),
                  pl.BlockSpec((8, dff), lambda i, j: (jnp.maximum(i * (tm // 8) - 1, 0), 0)),
                  pl.BlockSpec((8, dff), lambda i, j: (jnp.minimum((i + 1) * (tm // 8), nb8 - 1), 0)),
                  pl.BlockSpec((3, dff), lambda i, j: (0, 0)),
                  pl.BlockSpec((1, dff), lambda i, j: (0, 0)),
                  pl.BlockSpec((dff, tn), lambda i, j: (0, j)),
                  pl.BlockSpec((tm, tn), lambda i, j: (i, j)),
                  pl.BlockSpec((1, 6, tn), lambda i, j: (seg(i, tm), 0, j))],
        out_specs=pl.BlockSpec((tm, tn), lambda i, j: (i, j)),
        scratch_shapes=[pltpu.VMEM((tm, dff), BF16)],
        compiler_params=_cparams(2, 56),
    )(u, u, u, u, conv_w, conv_b.reshape(1, dff), w, xres, mod)


def _softmax_update(carry, s, v):
    m, l, acc = carry
    m_new = jnp.maximum(m, jnp.max(s, axis=1, keepdims=True))
    p = jnp.exp(s - m_new)
    alpha = jnp.exp(m - m_new)
    l = alpha * l + jnp.sum(p, axis=1, keepdims=True)
    acc = alpha * acc + _dot(p.astype(BF16), v)
    return m_new, l, acc


def _attn_kernel(*refs, mode, r_heads, n_lat, tk, has_sink, rows, nt):
    refs = list(refs)
    q_ref, kc_ref, vc_ref = refs[:3]
    pos = 3
    kl_ref = vl_ref = sink_ref = bias_ref = None
    if mode != "ctx":
        kl_ref, vl_ref = refs[pos:pos + 2]
        pos += 2
    if has_sink:
        sink_ref = refs[pos]
        pos += 1
    if mode == "na":
        bias_ref = refs[pos]
        pos += 1
    o_ref = refs[-1]
    g = pl.program_id(1)
    t = pl.program_id(2)
    tq = q_ref.shape[1]
    dq = q_ref.shape[2]
    m_rows = r_heads * tq
    q = q_ref[...].reshape(m_rows, dq)

    if has_sink:
        m0 = jnp.concatenate([jnp.full((tq, 1), sink_ref[g * r_heads + r], F32) for r in range(r_heads)], axis=0)
        l0 = jnp.ones((m_rows, 1), F32)
    else:
        m0 = jnp.full((m_rows, 1), NEG_INF, F32)
        l0 = jnp.zeros((m_rows, 1), F32)
    carry = (m0, l0, jnp.zeros((m_rows, vc_ref.shape[-1]), F32))
    carry = _softmax_update(carry, _dot_nt(q, kc_ref[0]), vc_ref[0])

    if mode == "dense":
        def body(c, carry):
            sl = pl.ds(pl.multiple_of(c * tk, tk), tk)
            return _softmax_update(carry, _dot_nt(q, kl_ref[0, sl, :]), vl_ref[0, sl, :])
        carry = lax.fori_loop(0, n_lat // tk, body, carry)
    elif mode == "window":
        span = tq + 2 * WINDOW
        ks = pl.multiple_of(jnp.clip(t * tq - WINDOW, 0, n_lat - span), WINDOW)
        sl = pl.ds(ks, span)
        s = _dot_nt(q, kl_ref[0, sl, :])
        row = lax.broadcasted_iota(jnp.int32, (m_rows, span), 0)
        col = lax.broadcasted_iota(jnp.int32, (m_rows, span), 1)
        d = (t * tq - ks) + (row & (tq - 1)) - col
        s = jnp.where((d <= WINDOW) & (d >= -WINDOW), s, NEG_INF)
        carry = _softmax_update(carry, s, vl_ref[0, sl, :])
    elif mode == "na":
        span = NA_KEY_ROWS * GRID_W
        q_rows = tq // GRID_W
        ks_row = jnp.clip(t * q_rows - NA_KH // 2, 0, rows - NA_KEY_ROWS)
        sl = pl.ds(pl.multiple_of(ks_row * GRID_W, GRID_W), span)
        variant = jnp.where(t == 0, 0, jnp.where(t == nt - 1, 2, 1))
        s = _dot_nt(q, kl_ref[0, sl, :]) + bias_ref[variant].reshape(m_rows, span)
        carry = _softmax_update(carry, s, vl_ref[0, sl, :])

    _, l, acc = carry
    o = acc / l
    dv = acc.shape[-1]
    for r in range(r_heads):
        o_ref[:, r * dv:(r + 1) * dv] = o[r * tq:(r + 1) * tq].astype(o_ref.dtype)


def _attention(qa, ka, va, o_prev, *, mode, batch, n_lat, n_ctx, groups, r_heads, q_base, k_base, v_base,
               sink=None, bias=None, tk=512):
    t_rows = qa.shape[1]
    dq = qa.shape[2]
    dv = va.shape[2]
    lat_rows = batch * n_lat
    nq = (n_ctx if mode == "ctx" else n_lat) // TQ
    ctx_blk = lat_rows // n_ctx
    rows = n_lat // GRID_W
    if mode == "ctx":
        qrow = lambda b, g, t: lat_rows // TQ + b * nq + t
    else:
        qrow = lambda b, g, t: b * nq + t
    in_specs = [pl.BlockSpec((r_heads, TQ, dq), lambda b, g, t: (q_base + g, qrow(b, g, t), 0)),
                pl.BlockSpec((1, n_ctx, dq), lambda b, g, t: (k_base + g, ctx_blk + b, 0)),
                pl.BlockSpec((1, n_ctx, dv), lambda b, g, t: (v_base + g, ctx_blk + b, 0))]
    args = [qa, ka, va]
    if mode != "ctx":
        in_specs += [pl.BlockSpec((1, n_lat, dq), lambda b, g, t: (k_base + g, b, 0)),
                     pl.BlockSpec((1, n_lat, dv), lambda b, g, t: (v_base + g, b, 0))]
        args += [ka, va]
    if sink is not None:
        in_specs.append(pl.BlockSpec(memory_space=pltpu.SMEM))
        args.append(sink)
    if mode == "na":
        in_specs.append(pl.BlockSpec((3, r_heads, TQ, NA_KEY_ROWS * GRID_W), lambda b, g, t: (0, g, 0, 0)))
        args.append(bias)
    aliases = {}
    if o_prev is not None:
        in_specs.append(pl.BlockSpec(memory_space=pl.ANY))
        args.append(o_prev)
        aliases = {len(args) - 1: 0}
    kern = functools.partial(_attn_kernel, mode=mode, r_heads=r_heads, n_lat=n_lat, tk=tk,
                             has_sink=sink is not None, rows=rows, nt=nq)

    def body(*refs):
        if o_prev is not None:
            refs = refs[:-2] + refs[-1:]
        kern(*refs)

    return pl.pallas_call(
        body,
        out_shape=jax.ShapeDtypeStruct((t_rows, groups * r_heads * dv), BF16),
        grid=(batch, groups, nq),
        in_specs=in_specs,
        out_specs=pl.BlockSpec((TQ, r_heads * dv), lambda b, g, t: (qrow(b, g, t), g)),
        input_output_aliases=aliases,
        compiler_params=_cparams(3, 48),
    )(*args)


def _na_variants(rows, q_rows):
    nt = rows // q_rows

    def structure(t):
        ks_row = int(np.clip(t * q_rows - NA_KH // 2, 0, rows - NA_KEY_ROWS))
        out = []
        for qr in range(q_rows):
            r = t * q_rows + qr
            rs = int(np.clip(r - NA_KH // 2, 0, rows - NA_KH))
            out.append([(rs <= ks_row + j < rs + NA_KH, ks_row + j - r + NA_KH - 1) for j in range(NA_KEY_ROWS)])
        return out

    reps = [structure(0), structure(1), structure(nt - 1)]
    for t in range(nt):
        want = reps[0] if t == 0 else reps[2] if t == nt - 1 else reps[1]
        got = structure(t)
        for wq, gq in zip(want, got):
            for (wv, wa), (gv, ga) in zip(wq, gq):
                assert wv == gv and (not wv or wa == ga), "neighbourhood tile variants do not cover this grid"
    return reps


def _na_bias_kernel(rb_ref, o_ref, *, variants):
    c = lax.broadcasted_iota(jnp.int32, (GRID_W, LANES), 0)
    lane = lax.broadcasted_iota(jnp.int32, (GRID_W, LANES), 1)
    kc = lane & (GRID_W - 1)
    cs = jnp.clip(c - NA_KW // 2, 0, GRID_W - NA_KW)
    col_ok = (kc >= cs) & (kc < cs + NA_KW)
    low = lane < GRID_W
    neg = jnp.full((GRID_W, LANES), NEG_INF, F32)
    for v, var in enumerate(variants):
        for qr, krows in enumerate(var):
            for jp in range(NA_KEY_ROWS // 2):
                halves = []
                for hf in range(2):
                    valid, a = krows[2 * jp + hf]
                    if valid:
                        x = jnp.broadcast_to(rb_ref[0, a:a + 1, :], (GRID_W, LANES))
                        shift = (LANES - (NA_KW - 1) + GRID_W * hf) % LANES
                        halves.append(pltpu.roll(x, shift, 1, stride=1, stride_axis=0))
                    else:
                        halves.append(neg)
                tile = jnp.where(col_ok, jnp.where(low, halves[0], halves[1]), NEG_INF)
                o_ref[v, 0, qr * GRID_W:(qr + 1) * GRID_W, jp * LANES:(jp + 1) * LANES] = tile


def _na_bias(rel_bias, rows):
    q_rows = TQ // GRID_W
    variants = _na_variants(rows, q_rows)
    nh, na, nb = rel_bias.shape
    rb = jnp.zeros((nh, 16, LANES), F32).at[:, :na, :nb].set(rel_bias)
    span = NA_KEY_ROWS * GRID_W
    return pl.pallas_call(
        functools.partial(_na_bias_kernel, variants=variants),
        out_shape=jax.ShapeDtypeStruct((3, nh, TQ, span), F32),
        grid=(nh,),
        in_specs=[pl.BlockSpec((1, 16, LANES), lambda h: (h, 0, 0))],
        out_specs=pl.BlockSpec((3, 1, TQ, span), lambda h: (0, h, 0, 0)),
        compiler_params=_cparams(1, 32),
    )(rb)


def _mla_dq_kernel(a_ref, w_ref, g_ref, o_ref):
    o_ref[...] = _rms(_dot(a_ref[...], w_ref[...]), g_ref[...]).astype(BF16)


def _mla_dq(h, w, g, tm=512):
    t, d = h.shape
    n = w.shape[1]
    return pl.pallas_call(
        _mla_dq_kernel,
        out_shape=jax.ShapeDtypeStruct((t, n), BF16),
        grid=(t // tm,),
        in_specs=[pl.BlockSpec((tm, d), lambda i: (i, 0)),
                  pl.BlockSpec((d, n), lambda i: (0, 0)),
                  pl.BlockSpec((1, n), lambda i: (0, 0))],
        out_specs=pl.BlockSpec((tm, n), lambda i: (i, 0)),
        compiler_params=_cparams(1, 48),
    )(h, w, g.reshape(1, n))


def _mla_uq_kernel(a_ref, w_ref, cos_ref, s1_ref, s2_ref, o_ref, *, hpt, scale):
    acc = _dot(a_ref[...], w_ref[...])
    for r in range(hpt):
        base = r * MLA_QK
        o_ref[r, :, :MLA_NOPE] = (acc[:, base:base + MLA_NOPE] * scale).astype(BF16)
        pe = _rope64(acc[:, base + MLA_NOPE:base + MLA_QK], cos_ref[...], s1_ref[...], s2_ref[...])
        o_ref[r, :, MLA_NOPE:] = (pe * scale).astype(BF16)


def _mla_uq(cq, w, tabs, *, scale, tm=512, tn=512):
    t, k = cq.shape
    n = w.shape[1]
    hpt = tn // MLA_QK
    cos, s1, s2, tab_idx = tabs
    tab_spec = pl.BlockSpec((tm, LANES), lambda i, j: (tab_idx(i, tm), 0))
    return pl.pallas_call(
        functools.partial(_mla_uq_kernel, hpt=hpt, scale=scale),
        out_shape=jax.ShapeDtypeStruct((n // MLA_QK, t, MLA_QK), BF16),
        grid=(t // tm, n // tn),
        in_specs=[pl.BlockSpec((tm, k), lambda i, j: (i, 0)),
                  pl.BlockSpec((k, tn), lambda i, j: (0, j)),
                  tab_spec, tab_spec, tab_spec],
        out_specs=pl.BlockSpec((hpt, tm, MLA_QK), lambda i, j: (j, i, 0)),
        compiler_params=_cparams(2, 40),
    )(cq, w, cos, s1, s2)


def _mla_dkv_kernel(a_ref, w_ref, g_ref, cos_ref, s1_ref, s2_ref, ckv_ref, kpe_ref, *, rank):
    acc = _dot(a_ref[...], w_ref[...])
    ckv_ref[...] = _rms(acc[:, :rank], g_ref[...]).astype(BF16)
    kpe_ref[...] = _rope64(acc[:, rank:], cos_ref[...], s1_ref[...], s2_ref[...]).astype(BF16)


def _mla_dkv(h, w, g, tabs, *, rank, tm=512):
    t, d = h.shape
    n = w.shape[1]
    cos, s1, s2, tab_idx = tabs
    tab_spec = pl.BlockSpec((tm, LANES), lambda i: (tab_idx(i, tm), 0))
    return pl.pallas_call(
        functools.partial(_mla_dkv_kernel, rank=rank),
        out_shape=(jax.ShapeDtypeStruct((t, rank), BF16), jax.ShapeDtypeStruct((t, LANES), BF16)),
        grid=(t // tm,),
        in_specs=[pl.BlockSpec((tm, d), lambda i: (i, 0)),
                  pl.BlockSpec((d, n), lambda i: (0, 0)),
                  pl.BlockSpec((1, rank), lambda i: (0, 0)),
                  tab_spec, tab_spec, tab_spec],
        out_specs=(pl.BlockSpec((tm, rank), lambda i: (i, 0)),
                   pl.BlockSpec((tm, LANES), lambda i: (i, 0))),
        compiler_params=_cparams(1, 40),
    )(h, w, g.reshape(1, rank), cos, s1, s2)


def _mla_ukv_kernel(a_ref, w_ref, kpe_ref, k_ref, v_ref, *, hpt):
    acc = _dot(a_ref[...], w_ref[...])
    for r in range(hpt):
        base = r * (MLA_NOPE + MLA_V)
        k_ref[r, :, :MLA_NOPE] = acc[:, base:base + MLA_NOPE].astype(BF16)
        k_ref[r, :, MLA_NOPE:] = kpe_ref[...]
        v_ref[r] = acc[:, base + MLA_NOPE:base + MLA_NOPE + MLA_V].astype(BF16)


def _mla_ukv(ckv, w, kpe, *, tm=512, tn=512):
    t, k = ckv.shape
    n = w.shape[1]
    hpt = tn // (MLA_NOPE + MLA_V)
    nh = n // (MLA_NOPE + MLA_V)
    return pl.pallas_call(
        functools.partial(_mla_ukv_kernel, hpt=hpt),
        out_shape=(jax.ShapeDtypeStruct((nh, t, MLA_QK), BF16), jax.ShapeDtypeStruct((nh, t, MLA_V), BF16)),
        grid=(t // tm, n // tn),
        in_specs=[pl.BlockSpec((tm, k), lambda i, j: (i, 0)),
                  pl.BlockSpec((k, tn), lambda i, j: (0, j)),
                  pl.BlockSpec((tm, LANES), lambda i, j: (i, 0))],
        out_specs=(pl.BlockSpec((hpt, tm, MLA_QK), lambda i, j: (j, i, 0)),
                   pl.BlockSpec((hpt, tm, MLA_V), lambda i, j: (j, i, 0))),
        compiler_params=_cparams(2, 40),
    )(ckv, w, kpe)


def _axial_angles(n_tok, rot_dim):
    n_freq = rot_dim // 4
    freqs = ROPE_THETA ** (-jnp.arange(n_freq, dtype=F32) / n_freq)
    t = jnp.arange(n_tok)
    row = (t // GRID_W).astype(F32)
    col = (t % GRID_W).astype(F32)
    return jnp.concatenate([row[:, None] * freqs, col[:, None] * freqs], axis=-1)


def _rope_tables(n_lat, tail):
    ang = _axial_angles(n_lat, HEAD_DIM)
    cos, sin = jnp.cos(ang), jnp.sin(ang)
    cos_t = jnp.concatenate([cos, cos], axis=-1)
    sin_t = jnp.concatenate([-sin, sin], axis=-1)
    cos_t = jnp.concatenate([cos_t, jnp.ones((tail, HEAD_DIM), F32)], axis=0)
    sin_t = jnp.concatenate([sin_t, jnp.zeros((tail, HEAD_DIM), F32)], axis=0)
    return cos_t, sin_t


def _rope64_tables(n_lat, tail):
    ang = _axial_angles(n_lat, MLA_ROPE)
    cos, sin = jnp.cos(ang), jnp.sin(ang)
    half = MLA_ROPE // 2
    z = jnp.zeros((n_lat, half), F32)
    zpad = jnp.zeros((n_lat, LANES - MLA_ROPE), F32)
    cos_t = jnp.concatenate([cos, cos, zpad], axis=-1)
    s1 = jnp.concatenate([z, sin, zpad], axis=-1)
    s2 = jnp.concatenate([-sin, z, zpad], axis=-1)
    ident = jnp.concatenate([jnp.ones((tail, MLA_ROPE), F32), jnp.zeros((tail, LANES - MLA_ROPE), F32)], axis=-1)
    zt = jnp.zeros((tail, LANES), F32)
    return (jnp.concatenate([cos_t, ident], axis=0), jnp.concatenate([s1, zt], axis=0),
            jnp.concatenate([s2, zt], axis=0))


def kernel(x, c, ctx, c_ctx, l0_ada_w, l0_ada_b, l0_attn_norm_g, l0_w_qkv, l0_q_norm_g, l0_k_norm_g, l0_w_o, l0_ffn_norm_g, l0_ffn_w_up, l0_ffn_conv_w, l0_ffn_conv_b, l0_ffn_w_down, l1_ada_w, l1_ada_b, l1_attn_norm_g, l1_w_qkv, l1_sink, l1_w_o, l1_ffn_norm_g, l1_ffn_w_up, l1_ffn_conv_w, l1_ffn_conv_b, l1_ffn_w_down, l2_ada_w, l2_ada_b, l2_attn_norm_g, l2_w_qkv, l2_rel_bias, l2_w_o, l2_ffn_norm_g, l2_ffn_w_up, l2_ffn_conv_w, l2_ffn_conv_b, l2_ffn_w_down, l3_ada_w, l3_ada_b, l3_attn_norm_g, l3_w_dq, l3_q_norm_g, l3_w_uq, l3_w_dkv, l3_kv_norm_g, l3_w_ukv, l3_w_o, l3_ffn_norm_g, l3_ffn_w_up, l3_ffn_conv_w, l3_ffn_conv_b, l3_ffn_w_down, final_norm_g):
    batch, n_lat, d = x.shape
    n_ctx = ctx.shape[1]
    lat_rows = batch * n_lat
    t_rows = lat_rows + batch * n_ctx
    rows = n_lat // GRID_W
    tm = 512
    assert batch + 1 <= 8 and n_ctx == TQ and n_lat % tm == 0 and (batch * n_ctx) % tm == 0
    assert n_lat & (n_lat - 1) == 0 and rows >= NA_KEY_ROWS and n_lat >= TQ + 2 * WINDOW

    def seg(i, tile):
        return jnp.minimum(i * tile // n_lat, batch)

    def tab_idx(i, tile):
        return jnp.where(i < lat_rows // tile, i % (n_lat // tile), n_lat // tile)

    cos_t, sin_t = _rope_tables(n_lat, tm)
    tabs128 = (cos_t, sin_t, tab_idx)
    tabs64 = _rope64_tables(n_lat, tm) + (tab_idx,)
    ones_g = jnp.ones((HEAD_DIM,), F32)

    xa = jnp.concatenate([x.reshape(lat_rows, d), ctx.reshape(batch * n_ctx, d)], axis=0)
    c8 = jnp.zeros((8, d), F32).at[:batch].set(c).at[batch].set(c_ctx)

    ada = [(l0_ada_w, l0_ada_b), (l1_ada_w, l1_ada_b), (l2_ada_w, l2_ada_b), (l3_ada_w, l3_ada_b)]
    attn_g = [l0_attn_norm_g, l1_attn_norm_g, l2_attn_norm_g, l3_attn_norm_g]
    ffn_g = [l0_ffn_norm_g, l1_ffn_norm_g, l2_ffn_norm_g, l3_ffn_norm_g]
    w_o = [l0_w_o, l1_w_o, l2_w_o, l3_w_o]
    ffn_p = [(l0_ffn_w_up, l0_ffn_conv_w, l0_ffn_conv_b, l0_ffn_w_down),
             (l1_ffn_w_up, l1_ffn_conv_w, l1_ffn_conv_b, l1_ffn_w_down),
             (l2_ffn_w_up, l2_ffn_conv_w, l2_ffn_conv_b, l2_ffn_w_down),
             (l3_ffn_w_up, l3_ffn_conv_w, l3_ffn_conv_b, l3_ffn_w_down)]
    w_qkv = [l0_w_qkv, l1_w_qkv, l2_w_qkv]
    gqa_scale = HEAD_DIM ** -0.5
    common = dict(batch=batch, n_lat=n_lat, n_ctx=n_ctx)
    gqa = dict(groups=N_KV_HEADS, r_heads=Q_PER_KV, q_base=0, k_base=N_HEADS, v_base=N_HEADS + N_KV_HEADS,
               **common)
    depth = 4

    for layer in range(depth):
        need_ctx = layer < depth - 1
        mod = _ada(c8, *ada[layer]).reshape(8, 6, d)
        h = _normmod(xa, attn_g[layer], mod, seg, rows=t_rows, shift_idx=0, scale_idx=1)

        if layer < 3:
            w = w_qkv[layer].astype(BF16)
            if layer == 0:
                qkv = _qkv_proj(h, w, tabs128, l0_q_norm_g, l0_k_norm_g, use_norm=True, use_rope=True,
                                scale=gqa_scale)
            elif layer == 1:
                qkv = _qkv_proj(h, w, tabs128, ones_g, ones_g, use_norm=False, use_rope=True, scale=gqa_scale)
            else:
                qkv = _qkv_proj(h, w, tabs128, ones_g, ones_g, use_norm=False, use_rope=False, scale=gqa_scale)
            sink = l1_sink if layer == 1 else None
            if layer == 0:
                o = _attention(qkv, qkv, qkv, None, mode="dense", **gqa)
            elif layer == 1:
                o = _attention(qkv, qkv, qkv, None, mode="window", sink=sink, **gqa)
            else:
                o = _attention(qkv, qkv, qkv, None, mode="na", bias=_na_bias(l2_rel_bias, rows), **gqa)
            o = _attention(qkv, qkv, qkv, o, mode="ctx", sink=sink, **gqa)
        else:
            mla_scale = (MLA_NOPE + MLA_ROPE) ** -0.5
            rq = l3_w_uq.shape[0]
            w_uq = l3_w_uq.reshape(rq, N_HEADS, MLA_NOPE + MLA_ROPE)
            w_uq = jnp.pad(w_uq, ((0, 0), (0, 0), (0, MLA_QK - MLA_NOPE - MLA_ROPE))).reshape(rq, N_HEADS * MLA_QK)
            rank = l3_kv_norm_g.shape[0]
            w_dkv = jnp.pad(l3_w_dkv, ((0, 0), (0, LANES - MLA_ROPE)))
            cq = _mla_dq(h, l3_w_dq.astype(BF16), l3_q_norm_g)
            qh = _mla_uq(cq, w_uq.astype(BF16), tabs64, scale=mla_scale)
            ckv, kpe = _mla_dkv(h, w_dkv.astype(BF16), l3_kv_norm_g, tabs64, rank=rank)
            kh, vh = _mla_ukv(ckv, l3_w_ukv.astype(BF16), kpe)
            o = _attention(qh, kh, vh, None, mode="dense", groups=N_HEADS, r_heads=1, q_base=0, k_base=0,
                           v_base=0, **common)

        out_rows = t_rows if need_ctx else lat_rows
        xa = _proj_res(o, w_o[layer].astype(BF16), xa, mod, seg, rows=out_rows, gate_idx=2)
        h2 = _normmod(xa, ffn_g[layer], mod, seg, rows=out_rows, shift_idx=3, scale_idx=4)
        w_up, conv_w, conv_b, w_down = ffn_p[layer]
        u = _proj_plain(h2, w_up.astype(BF16), rows=out_rows)
        xa = _ffn_down(u, conv_w, conv_b, w_down.astype(BF16), xa, mod, seg, rows=out_rows, n_lat=n_lat,
                       n_ctx=n_ctx, lat_rows=lat_rows, gate_idx=5)

    return _final_norm(xa, final_norm_g, rows=lat_rows).reshape(batch, n_lat, d)
```

```python
import functools
import math

import jax
import jax.numpy as jnp
import numpy as np
from jax import lax
from jax.experimental import pallas as pl
from jax.experimental.pallas import tpu as pltpu

F32 = jnp.float32
BF16 = jnp.bfloat16

GRID_W = 64
HEAD_DIM = 128
N_HEADS = 32
N_KV_HEADS = 8
Q_PER_KV = 4
WINDOW = 128
NA_KH = 8
NA_KW = 16
NA_KEY_ROWS = 12
MLA_NOPE = 128
MLA_ROPE = 64
MLA_V = 128
MLA_QK = 256
ROPE_THETA = 10000.0
EPS = 1e-6
NEG_INF = -1e30
LANES = 128
TQ = 256
KV_UNROLL = 8
MIB = 1024 * 1024


def _cparams(n_axes, vmem_mib):
    return pltpu.CompilerParams(dimension_semantics=("arbitrary",) * n_axes,
                                vmem_limit_bytes=vmem_mib * MIB)


def _dot(a, b):
    return jnp.dot(a, b, preferred_element_type=F32)


def _dot_nt(a, b):
    return lax.dot_general(a, b, (((1,), (1,)), ((), ())), preferred_element_type=F32)


def _rms(y, g):
    return y * lax.rsqrt(jnp.mean(y * y, axis=-1, keepdims=True) + EPS) * g


def _ada_kernel(c_ref, w_ref, b_ref, o_ref):
    x = c_ref[...]
    s = x / (1.0 + jnp.exp(-x))
    o_ref[...] = _dot(s.astype(BF16), w_ref[...].astype(BF16)) + b_ref[...]


def _ada(c8, w, b, tn=512):
    d, n = w.shape
    return pl.pallas_call(
        _ada_kernel,
        out_shape=jax.ShapeDtypeStruct((8, n), F32),
        grid=(n // tn,),
        in_specs=[pl.BlockSpec((8, d), lambda j: (0, 0)),
                  pl.BlockSpec((d, tn), lambda j: (0, j)),
                  pl.BlockSpec((1, tn), lambda j: (0, j))],
        out_specs=pl.BlockSpec((8, tn), lambda j: (0, j)),
        compiler_params=_cparams(1, 40),
        name="ada",
    )(c8, w, b.reshape(1, n))


def _normmod_kernel(x_ref, g_ref, mod_ref, o_ref, *, shift_idx, scale_idx):
    y = _rms(x_ref[...], g_ref[...])
    y = y * (1.0 + mod_ref[0, scale_idx:scale_idx + 1, :]) + mod_ref[0, shift_idx:shift_idx + 1, :]
    o_ref[...] = y.astype(o_ref.dtype)


def _normmod(x, g, mod, seg, *, rows, shift_idx, scale_idx, tm=256):
    d = x.shape[1]
    return pl.pallas_call(
        functools.partial(_normmod_kernel, shift_idx=shift_idx, scale_idx=scale_idx),
        out_shape=jax.ShapeDtypeStruct((rows, d), BF16),
        grid=(rows // tm,),
        in_specs=[pl.BlockSpec((tm, d), lambda i: (i, 0)),
                  pl.BlockSpec((1, d), lambda i: (0, 0)),
                  pl.BlockSpec((1, 6, d), lambda i: (seg(i, tm), 0, 0))],
        out_specs=pl.BlockSpec((tm, d), lambda i: (i, 0)),
        compiler_params=_cparams(1, 32),
        name="normmod",
    )(x, g.reshape(1, d), mod)


def _final_norm_kernel(x_ref, g_ref, o_ref):
    o_ref[...] = _rms(x_ref[...], g_ref[...])


def _final_norm(x, g, *, rows, tm=256):
    d = x.shape[1]
    return pl.pallas_call(
        _final_norm_kernel,
        out_shape=jax.ShapeDtypeStruct((rows, d), F32),
        grid=(rows // tm,),
        in_specs=[pl.BlockSpec((tm, d), lambda i: (i, 0)),
                  pl.BlockSpec((1, d), lambda i: (0, 0))],
        out_specs=pl.BlockSpec((tm, d), lambda i: (i, 0)),
        compiler_params=_cparams(1, 32),
        name="final_norm",
    )(x, g.reshape(1, d))


def _rope128(y, cos, sin):
    return y * cos + pltpu.roll(y, HEAD_DIM // 2, 1) * sin


def _rope64(y, cos, s1, s2):
    half = MLA_ROPE // 2
    return y * cos + pltpu.roll(y, half, 1) * s1 + pltpu.roll(y, LANES - half, 1) * s2


def _qkv_kernel(a_ref, w_ref, cos_ref, sin_ref, gq_ref, gk_ref, o_ref, *,
                nq_tiles, nqk_tiles, hpt, use_norm, use_rope, scale):
    j = pl.program_id(1)
    acc = _dot(a_ref[...], w_ref[...])

    @pl.when(j < nqk_tiles)
    def _():
        is_q = j < nq_tiles
        fac = jnp.where(is_q, scale, 1.0).astype(F32)
        g = jnp.where(is_q, gq_ref[...], gk_ref[...])
        for r in range(hpt):
            y = acc[:, r * HEAD_DIM:(r + 1) * HEAD_DIM]
            if use_norm:
                y = _rms(y, g)
            if use_rope:
                y = _rope128(y, cos_ref[...], sin_ref[...])
            o_ref[r] = (y * fac).astype(BF16)

    @pl.when(j >= nqk_tiles)
    def _():
        for r in range(hpt):
            o_ref[r] = acc[:, r * HEAD_DIM:(r + 1) * HEAD_DIM].astype(BF16)


def _qkv_proj(h, w, tabs, gq, gk, *, use_norm, use_rope, scale, tm=512, tn=512):
    t, d = h.shape
    n = w.shape[1]
    hpt = tn // HEAD_DIM
    cos, sin, tab_idx = tabs
    kern = functools.partial(
        _qkv_kernel, nq_tiles=N_HEADS * HEAD_DIM // tn, nqk_tiles=(N_HEADS + N_KV_HEADS) * HEAD_DIM // tn,
        hpt=hpt, use_norm=use_norm, use_rope=use_rope, scale=scale)
    return pl.pallas_call(
        kern,
        out_shape=jax.ShapeDtypeStruct((n // HEAD_DIM, t, HEAD_DIM), BF16),
        grid=(t // tm, n // tn),
        in_specs=[pl.BlockSpec((tm, d), lambda i, j: (i, 0)),
                  pl.BlockSpec((d, tn), lambda i, j: (0, j)),
                  pl.BlockSpec((tm, HEAD_DIM), lambda i, j: (tab_idx(i, tm), 0)),
                  pl.BlockSpec((tm, HEAD_DIM), lambda i, j: (tab_idx(i, tm), 0)),
                  pl.BlockSpec((1, HEAD_DIM), lambda i, j: (0, 0)),
                  pl.BlockSpec((1, HEAD_DIM), lambda i, j: (0, 0))],
        out_specs=pl.BlockSpec((hpt, tm, HEAD_DIM), lambda i, j: (j, i, 0)),
        compiler_params=_cparams(2, 40),
        name="qkv",
    )(h, w, cos, sin, gq.reshape(1, HEAD_DIM), gk.reshape(1, HEAD_DIM))


def _res_kernel(a_ref, w_ref, x_ref, mod_ref, o_ref, *, gate_idx):
    acc = _dot(a_ref[...], w_ref[...])
    o_ref[...] = x_ref[...] + mod_ref[0, gate_idx:gate_idx + 1, :] * acc


def _proj_res(a, w, xres, mod, seg, *, rows, gate_idx, tm=512, tn=512):
    k, n = w.shape
    return pl.pallas_call(
        functools.partial(_res_kernel, gate_idx=gate_idx),
        out_shape=jax.ShapeDtypeStruct((rows, n), F32),
        grid=(rows // tm, n // tn),
        in_specs=[pl.BlockSpec((tm, k), lambda i, j: (i, 0)),
                  pl.BlockSpec((k, tn), lambda i, j: (0, j)),
                  pl.BlockSpec((tm, tn), lambda i, j: (i, j)),
                  pl.BlockSpec((1, 6, tn), lambda i, j: (seg(i, tm), 0, j))],
        out_specs=pl.BlockSpec((tm, tn), lambda i, j: (i, j)),
        compiler_params=_cparams(2, 40),
        name="proj_res",
    )(a, w, xres, mod)


def _plain_kernel(a_ref, w_ref, o_ref):
    o_ref[...] = _dot(a_ref[...], w_ref[...]).astype(o_ref.dtype)


def _proj_plain(a, w, *, rows, tm=512, tn=512):
    k, n = w.shape
    return pl.pallas_call(
        _plain_kernel,
        out_shape=jax.ShapeDtypeStruct((rows, n), BF16),
        grid=(rows // tm, n // tn),
        in_specs=[pl.BlockSpec((tm, k), lambda i, j: (i, 0)),
                  pl.BlockSpec((k, tn), lambda i, j: (0, j))],
        out_specs=pl.BlockSpec((tm, tn), lambda i, j: (i, j)),
        compiler_params=_cparams(2, 40),
        name="proj_plain",
    )(a, w)


def _ffn_down_kernel(a_ref, v_ref, ap_ref, an_ref, cw_ref, cb_ref, w_ref, x_ref, mod_ref, o_ref, g_scr, *,
                     tm, n_chunks, n_lat, n_ctx, lat_rows, gate_idx):
    i = pl.program_id(0)
    j = pl.program_id(1)

    @pl.when(j == 0)
    def _():
        row = lax.broadcasted_iota(jnp.int32, (tm, LANES), 0)
        grow = i * tm + row
        is_ctx = grow >= lat_rows
        first = ((grow & (n_ctx - 1)) == 0) & (is_ctx | ((grow & (n_lat - 1)) == 0))
        nxt = grow + 1
        last = ((nxt & (n_ctx - 1)) == 0) & (is_ctx | ((nxt & (n_lat - 1)) == 0))
        top = row == 0
        bot = row == tm - 1

        def chunk(c, carry):
            off = pl.multiple_of(c * LANES, LANES)
            sl = pl.ds(off, LANES)
            a = a_ref[:, sl].astype(F32)
            prev_row = ap_ref[7:8, sl].astype(F32)
            next_row = an_ref[0:1, sl].astype(F32)
            a_dn = jnp.where(top, prev_row, pltpu.roll(a, 1, 0))
            a_dn = jnp.where(first, 0.0, a_dn)
            a_up = jnp.where(bot, next_row, pltpu.roll(a, tm - 1, 0))
            a_up = jnp.where(last, 0.0, a_up)
            cv = cw_ref[0:1, sl] * a_dn + cw_ref[1:2, sl] * a + cw_ref[2:3, sl] * a_up + cb_ref[:, sl]
            gv = cv / (1.0 + jnp.exp(-cv)) * v_ref[:, sl].astype(F32)
            g_scr[:, sl] = gv.astype(BF16)
            return carry

        lax.fori_loop(0, n_chunks, chunk, 0)

    acc = _dot(g_scr[...], w_ref[...])
    o_ref[...] = x_ref[...] + mod_ref[0, gate_idx:gate_idx + 1, :] * acc


def _ffn_down(u, conv_w, conv_b, w, xres, mod, seg, *, rows, n_lat, n_ctx, lat_rows, gate_idx, tm=512, tn=512):
    dff, n = w.shape
    nb8 = u.shape[0] // 8
    kern = functools.partial(_ffn_down_kernel, tm=tm, n_chunks=dff // LANES, n_lat=n_lat, n_ctx=n_ctx,
                             lat_rows=lat_rows, gate_idx=gate_idx)
    return pl.pallas_call(
        kern,
        out_shape=jax.ShapeDtypeStruct((rows, n), F32),
        grid=(rows // tm, n // tn),
        in_specs=[pl.BlockSpec((tm, dff), lambda i, j: (i, 0)),
                  pl.BlockSpec((tm, dff), lambda i, j: (i, 1)),
                  pl.BlockSpec((8, dff), lambda i, j: (jnp.maximum(i * (tm // 8) - 1, 0), 0)),
                  pl.BlockSpec((8, dff), lambda i, j: (jnp.minimum((i + 1) * (tm // 8), nb8 - 1), 0)),
                  pl.BlockSpec((3, dff), lambda i, j: (0, 0)),
                  pl.BlockSpec((1, dff), lambda i, j: (0, 0)),
                  pl.BlockSpec((dff, tn), lambda i, j: (0, j)),
                  pl.BlockSpec((tm, tn), lambda i, j: (i, j)),
                  pl.BlockSpec((1, 6, tn), lambda i, j: (seg(i, tm), 0, j))],
        out_specs=pl.BlockSpec((tm, tn), lambda i, j: (i, j)),
        scratch_shapes=[pltpu.VMEM((tm, dff), BF16)],
        compiler_params=_cparams(2, 56),
        name="ffn_down",
    )(u, u, u, u, conv_w, conv_b.reshape(1, dff), w, xres, mod)


def _softmax_first(s, v, sink_tile):
    n = s.shape[1]
    if sink_tile is not None:
        s = jnp.concatenate([s, sink_tile], axis=1)
    m = jnp.max(s, axis=1, keepdims=True)
    p = jnp.exp(s - m)
    return m, jnp.sum(p, axis=1, keepdims=True), _dot(p[:, :n].astype(BF16), v)


def _softmax_empty(rows, dv):
    return jnp.full((rows, 1), NEG_INF, F32), jnp.zeros((rows, 1), F32), jnp.zeros((rows, dv), F32)


def _softmax_update(carry, s, v, sink_tile=None):
    m, l, acc = carry
    n = s.shape[1]
    if sink_tile is not None:
        s = jnp.concatenate([s, sink_tile], axis=1)
    m_new = jnp.maximum(m, jnp.max(s, axis=1, keepdims=True))
    p = jnp.exp(s - m_new)
    alpha = jnp.exp(m - m_new)
    return m_new, alpha * l + jnp.sum(p, axis=1, keepdims=True), alpha * acc + _dot(p[:, :n].astype(BF16), v)


def _attn_kernel(*refs, mode, r_heads, n_lat, tk, has_sink, rows, nt):
    refs = list(refs)
    q_ref, kc_ref, vc_ref = refs[:3]
    pos = 3
    kl_ref = vl_ref = sink_ref = bias_ref = None
    if mode != "ctx":
        kl_ref, vl_ref = refs[pos:pos + 2]
        pos += 2
    if has_sink:
        sink_ref = refs[pos]
        pos += 1
    if mode in ("window", "na"):
        bias_ref = refs[pos]
        pos += 1
    o_ref = refs[-1]
    g = pl.program_id(1)
    t = pl.program_id(2)
    tq = q_ref.shape[1]
    dq = q_ref.shape[2]
    m_rows = r_heads * tq
    q = q_ref[...].reshape(m_rows, dq)

    sink_col = None
    if has_sink:
        sink_col = jnp.concatenate([jnp.broadcast_to(sink_ref[r], (tq, LANES)) for r in range(r_heads)], axis=0)
    s_ctx = _dot_nt(q, kc_ref[0])

    if mode == "ctx":
        carry = _softmax_first(s_ctx, vc_ref[0], sink_col)
    elif mode == "dense":
        n_chunks = n_lat // tk
        unroll = math.gcd(KV_UNROLL, n_chunks)

        def chunk(c):
            return pl.ds(pl.multiple_of(c * tk, tk), tk)

        carry = _softmax_first(s_ctx, vc_ref[0], sink_col)

        def body(c, carry):
            scores = [_dot_nt(q, kl_ref[0, chunk(unroll * c + u), :]) for u in range(unroll)]
            for u in range(unroll):
                carry = _softmax_update(carry, scores[u], vl_ref[0, chunk(unroll * c + u), :])
            return carry

        carry = lax.fori_loop(0, n_chunks // unroll, body, carry)
    else:
        variant = jnp.where(t == 0, 0, jnp.where(t == nt - 1, 2, 1))
        if mode == "window":
            span = tq + 2 * WINDOW
            start = pl.multiple_of(jnp.clip(t * tq - WINDOW, 0, n_lat - span), WINDOW)
            bias = jnp.concatenate([bias_ref[variant]] * r_heads, axis=0)
        else:
            span = NA_KEY_ROWS * GRID_W
            ks_row = jnp.clip(t * (tq // GRID_W) - NA_KH // 2, 0, rows - NA_KEY_ROWS)
            start = pl.multiple_of(ks_row * GRID_W, GRID_W)
            bias = bias_ref[variant].reshape(m_rows, span)
        sl = pl.ds(start, span)
        s_loc = _dot_nt(q, kl_ref[0, sl, :]) + bias
        s_all = jnp.concatenate([s_ctx, s_loc], axis=1)
        v_all = jnp.concatenate([vc_ref[0], vl_ref[0, sl, :]], axis=0)
        carry = _softmax_update(_softmax_empty(m_rows, v_all.shape[1]), s_all, v_all, sink_col)

    _, l, acc = carry
    o = acc / l
    dv = acc.shape[-1]
    for r in range(r_heads):
        o_ref[:, r * dv:(r + 1) * dv] = o[r * tq:(r + 1) * tq].astype(o_ref.dtype)


def _window_bias():
    span = TQ + 2 * WINDOW
    d = np.arange(TQ)[:, None] - np.arange(span)[None, :]
    offsets = (0, WINDOW, span - TQ)
    return np.stack([np.where(np.abs(d + o) <= WINDOW, 0.0, NEG_INF) for o in offsets]).astype(np.float32)


def _attention(qa, ka, va, o_prev, *, mode, batch, n_lat, n_ctx, groups, r_heads, q_base, k_base, v_base,
               sink=None, bias=None, tq=TQ, tk=512):
    t_rows = qa.shape[1]
    dq = qa.shape[2]
    dv = va.shape[2]
    lat_rows = batch * n_lat
    nq = (n_ctx if mode == "ctx" else n_lat) // tq
    ctx_blk = lat_rows // n_ctx
    rows = n_lat // GRID_W
    if mode == "ctx":
        qrow = lambda b, g, t: lat_rows // tq + b * nq + t
    else:
        qrow = lambda b, g, t: b * nq + t
    in_specs = [pl.BlockSpec((r_heads, tq, dq), lambda b, g, t: (q_base + g, qrow(b, g, t), 0)),
                pl.BlockSpec((1, n_ctx, dq), lambda b, g, t: (k_base + g, ctx_blk + b, 0)),
                pl.BlockSpec((1, n_ctx, dv), lambda b, g, t: (v_base + g, ctx_blk + b, 0))]
    args = [qa, ka, va]
    if mode != "ctx":
        in_specs += [pl.BlockSpec((1, n_lat, dq), lambda b, g, t: (k_base + g, b, 0)),
                     pl.BlockSpec((1, n_lat, dv), lambda b, g, t: (v_base + g, b, 0))]
        args += [ka, va]
    if sink is not None:
        in_specs.append(pl.BlockSpec((r_heads, 1, LANES), lambda b, g, t: (g, 0, 0)))
        args.append(jnp.full((sink.shape[0], 1, LANES), NEG_INF, F32).at[:, 0, 0].set(sink))
    if mode == "window":
        in_specs.append(pl.BlockSpec(bias.shape, lambda b, g, t: (0, 0, 0)))
        args.append(bias)
    if mode == "na":
        in_specs.append(pl.BlockSpec((3, r_heads, tq, NA_KEY_ROWS * GRID_W), lambda b, g, t: (0, g, 0, 0)))
        args.append(bias)
    aliases = {}
    if o_prev is not None:
        in_specs.append(pl.BlockSpec(memory_space=pl.ANY))
        args.append(o_prev)
        aliases = {len(args) - 1: 0}
    kern = functools.partial(_attn_kernel, mode=mode, r_heads=r_heads, n_lat=n_lat, tk=tk,
                             has_sink=sink is not None, rows=rows, nt=nq)

    def body(*refs):
        if o_prev is not None:
            refs = refs[:-2] + refs[-1:]
        kern(*refs)

    return pl.pallas_call(
        body,
        out_shape=jax.ShapeDtypeStruct((t_rows, groups * r_heads * dv), BF16),
        grid=(batch, groups, nq),
        in_specs=in_specs,
        out_specs=pl.BlockSpec((tq, r_heads * dv), lambda b, g, t: (qrow(b, g, t), g)),
        input_output_aliases=aliases,
        compiler_params=_cparams(3, 48),
        name="attn_" + mode,
    )(*args)


def _na_variants(rows, q_rows):
    nt = rows // q_rows

    def structure(t):
        ks_row = int(np.clip(t * q_rows - NA_KH // 2, 0, rows - NA_KEY_ROWS))
        out = []
        for qr in range(q_rows):
            r = t * q_rows + qr
            rs = int(np.clip(r - NA_KH // 2, 0, rows - NA_KH))
            out.append([(rs <= ks_row + j < rs + NA_KH, ks_row + j - r + NA_KH - 1) for j in range(NA_KEY_ROWS)])
        return out

    reps = [structure(0), structure(1), structure(nt - 1)]
    for t in range(nt):
        want = reps[0] if t == 0 else reps[2] if t == nt - 1 else reps[1]
        got = structure(t)
        for wq, gq in zip(want, got):
            for (wv, wa), (gv, ga) in zip(wq, gq):
                assert wv == gv and (not wv or wa == ga), "neighbourhood tile variants do not cover this grid"
    return reps


def _na_bias_kernel(rb_ref, o_ref, *, variants):
    c = lax.broadcasted_iota(jnp.int32, (GRID_W, LANES), 0)
    lane = lax.broadcasted_iota(jnp.int32, (GRID_W, LANES), 1)
    kc = lane & (GRID_W - 1)
    cs = jnp.clip(c - NA_KW // 2, 0, GRID_W - NA_KW)
    col_ok = (kc >= cs) & (kc < cs + NA_KW)
    low = lane < GRID_W
    neg = jnp.full((GRID_W, LANES), NEG_INF, F32)
    for v, var in enumerate(variants):
        for qr, krows in enumerate(var):
            for jp in range(NA_KEY_ROWS // 2):
                halves = []
                for hf in range(2):
                    valid, a = krows[2 * jp + hf]
                    if valid:
                        x = jnp.broadcast_to(rb_ref[0, a:a + 1, :], (GRID_W, LANES))
                        shift = (LANES - (NA_KW - 1) + GRID_W * hf) % LANES
                        halves.append(pltpu.roll(x, shift, 1, stride=1, stride_axis=0))
                    else:
                        halves.append(neg)
                tile = jnp.where(col_ok, jnp.where(low, halves[0], halves[1]), NEG_INF)
                o_ref[v, 0, qr * GRID_W:(qr + 1) * GRID_W, jp * LANES:(jp + 1) * LANES] = tile


def _na_bias(rel_bias, rows):
    q_rows = TQ // GRID_W
    variants = _na_variants(rows, q_rows)
    nh, na, nb = rel_bias.shape
    rb = jnp.zeros((nh, 16, LANES), F32).at[:, :na, :nb].set(rel_bias)
    span = NA_KEY_ROWS * GRID_W
    return pl.pallas_call(
        functools.partial(_na_bias_kernel, variants=variants),
        out_shape=jax.ShapeDtypeStruct((3, nh, TQ, span), F32),
        grid=(nh,),
        in_specs=[pl.BlockSpec((1, 16, LANES), lambda h: (h, 0, 0))],
        out_specs=pl.BlockSpec((3, 1, TQ, span), lambda h: (0, h, 0, 0)),
        compiler_params=_cparams(1, 32),
        name="na_bias",
    )(rb)


def _mla_dq_kernel(a_ref, w_ref, g_ref, o_ref):
    o_ref[...] = _rms(_dot(a_ref[...], w_ref[...]), g_ref[...]).astype(BF16)


def _mla_dq(h, w, g, tm=512):
    t, d = h.shape
    n = w.shape[1]
    return pl.pallas_call(
        _mla_dq_kernel,
        out_shape=jax.ShapeDtypeStruct((t, n), BF16),
        grid=(t // tm,),
        in_specs=[pl.BlockSpec((tm, d), lambda i: (i, 0)),
                  pl.BlockSpec((d, n), lambda i: (0, 0)),
                  pl.BlockSpec((1, n), lambda i: (0, 0))],
        out_specs=pl.BlockSpec((tm, n), lambda i: (i, 0)),
        compiler_params=_cparams(1, 48),
        name="mla_dq",
    )(h, w, g.reshape(1, n))


def _mla_uq_kernel(a_ref, w_ref, cos_ref, s1_ref, s2_ref, o_ref, *, hpt, scale):
    acc = _dot(a_ref[...], w_ref[...])
    for r in range(hpt):
        base = r * MLA_QK
        o_ref[r, :, :MLA_NOPE] = (acc[:, base:base + MLA_NOPE] * scale).astype(BF16)
        pe = _rope64(acc[:, base + MLA_NOPE:base + MLA_QK], cos_ref[...], s1_ref[...], s2_ref[...])
        o_ref[r, :, MLA_NOPE:] = (pe * scale).astype(BF16)


def _mla_uq(cq, w, tabs, *, scale, tm=512, tn=512):
    t, k = cq.shape
    n = w.shape[1]
    hpt = tn // MLA_QK
    cos, s1, s2, tab_idx = tabs
    tab_spec = pl.BlockSpec((tm, LANES), lambda i, j: (tab_idx(i, tm), 0))
    return pl.pallas_call(
        functools.partial(_mla_uq_kernel, hpt=hpt, scale=scale),
        out_shape=jax.ShapeDtypeStruct((n // MLA_QK, t, MLA_QK), BF16),
        grid=(t // tm, n // tn),
        in_specs=[pl.BlockSpec((tm, k), lambda i, j: (i, 0)),
                  pl.BlockSpec((k, tn), lambda i, j: (0, j)),
                  tab_spec, tab_spec, tab_spec],
        out_specs=pl.BlockSpec((hpt, tm, MLA_QK), lambda i, j: (j, i, 0)),
        compiler_params=_cparams(2, 40),
        name="mla_uq",
    )(cq, w, cos, s1, s2)


def _mla_dkv_kernel(a_ref, w_ref, g_ref, cos_ref, s1_ref, s2_ref, ckv_ref, kpe_ref, *, rank):
    acc = _dot(a_ref[...], w_ref[...])
    ckv_ref[...] = _rms(acc[:, :rank], g_ref[...]).astype(BF16)
    kpe_ref[...] = _rope64(acc[:, rank:], cos_ref[...], s1_ref[...], s2_ref[...]).astype(BF16)


def _mla_dkv(h, w, g, tabs, *, rank, tm=512):
    t, d = h.shape
    n = w.shape[1]
    cos, s1, s2, tab_idx = tabs
    tab_spec = pl.BlockSpec((tm, LANES), lambda i: (tab_idx(i, tm), 0))
    return pl.pallas_call(
        functools.partial(_mla_dkv_kernel, rank=rank),
        out_shape=(jax.ShapeDtypeStruct((t, rank), BF16), jax.ShapeDtypeStruct((t, LANES), BF16)),
        grid=(t // tm,),
        in_specs=[pl.BlockSpec((tm, d), lambda i: (i, 0)),
                  pl.BlockSpec((d, n), lambda i: (0, 0)),
                  pl.BlockSpec((1, rank), lambda i: (0, 0)),
                  tab_spec, tab_spec, tab_spec],
        out_specs=(pl.BlockSpec((tm, rank), lambda i: (i, 0)),
                   pl.BlockSpec((tm, LANES), lambda i: (i, 0))),
        compiler_params=_cparams(1, 40),
        name="mla_dkv",
    )(h, w, g.reshape(1, rank), cos, s1, s2)


def _mla_ukv_kernel(a_ref, w_ref, kpe_ref, k_ref, v_ref, *, hpt):
    acc = _dot(a_ref[...], w_ref[...])
    for r in range(hpt):
        base = r * (MLA_NOPE + MLA_V)
        k_ref[r, :, :MLA_NOPE] = acc[:, base:base + MLA_NOPE].astype(BF16)
        k_ref[r, :, MLA_NOPE:] = kpe_ref[...]
        v_ref[r] = acc[:, base + MLA_NOPE:base + MLA_NOPE + MLA_V].astype(BF16)


def _mla_ukv(ckv, w, kpe, *, tm=512, tn=512):
    t, k = ckv.shape
    n = w.shape[1]
    hpt = tn // (MLA_NOPE + MLA_V)
    nh = n // (MLA_NOPE + MLA_V)
    return pl.pallas_call(
        functools.partial(_mla_ukv_kernel, hpt=hpt),
        out_shape=(jax.ShapeDtypeStruct((nh, t, MLA_QK), BF16), jax.ShapeDtypeStruct((nh, t, MLA_V), BF16)),
        grid=(t // tm, n // tn),
        in_specs=[pl.BlockSpec((tm, k), lambda i, j: (i, 0)),
                  pl.BlockSpec((k, tn), lambda i, j: (0, j)),
                  pl.BlockSpec((tm, LANES), lambda i, j: (i, 0))],
        out_specs=(pl.BlockSpec((hpt, tm, MLA_QK), lambda i, j: (j, i, 0)),
                   pl.BlockSpec((hpt, tm, MLA_V), lambda i, j: (j, i, 0))),
        compiler_params=_cparams(2, 40),
        name="mla_ukv",
    )(ckv, w, kpe)


def _axial_angles(n_tok, rot_dim):
    n_freq = rot_dim // 4
    freqs = ROPE_THETA ** (-jnp.arange(n_freq, dtype=F32) / n_freq)
    t = jnp.arange(n_tok)
    row = (t // GRID_W).astype(F32)
    col = (t % GRID_W).astype(F32)
    return jnp.concatenate([row[:, None] * freqs, col[:, None] * freqs], axis=-1)


def _rope_tables(n_lat, tail):
    ang = _axial_angles(n_lat, HEAD_DIM)
    cos, sin = jnp.cos(ang), jnp.sin(ang)
    cos_t = jnp.concatenate([cos, cos], axis=-1)
    sin_t = jnp.concatenate([-sin, sin], axis=-1)
    cos_t = jnp.concatenate([cos_t, jnp.ones((tail, HEAD_DIM), F32)], axis=0)
    sin_t = jnp.concatenate([sin_t, jnp.zeros((tail, HEAD_DIM), F32)], axis=0)
    return cos_t, sin_t


def _rope64_tables(n_lat, tail):
    ang = _axial_angles(n_lat, MLA_ROPE)
    cos, sin = jnp.cos(ang), jnp.sin(ang)
    half = MLA_ROPE // 2
    z = jnp.zeros((n_lat, half), F32)
    zpad = jnp.zeros((n_lat, LANES - MLA_ROPE), F32)
    cos_t = jnp.concatenate([cos, cos, zpad], axis=-1)
    s1 = jnp.concatenate([z, sin, zpad], axis=-1)
    s2 = jnp.concatenate([-sin, z, zpad], axis=-1)
    ident = jnp.concatenate([jnp.ones((tail, MLA_ROPE), F32), jnp.zeros((tail, LANES - MLA_ROPE), F32)], axis=-1)
    zt = jnp.zeros((tail, LANES), F32)
    return (jnp.concatenate([cos_t, ident], axis=0), jnp.concatenate([s1, zt], axis=0),
            jnp.concatenate([s2, zt], axis=0))


def kernel(x, c, ctx, c_ctx, l0_ada_w, l0_ada_b, l0_attn_norm_g, l0_w_qkv, l0_q_norm_g, l0_k_norm_g, l0_w_o, l0_ffn_norm_g, l0_ffn_w_up, l0_ffn_conv_w, l0_ffn_conv_b, l0_ffn_w_down, l1_ada_w, l1_ada_b, l1_attn_norm_g, l1_w_qkv, l1_sink, l1_w_o, l1_ffn_norm_g, l1_ffn_w_up, l1_ffn_conv_w, l1_ffn_conv_b, l1_ffn_w_down, l2_ada_w, l2_ada_b, l2_attn_norm_g, l2_w_qkv, l2_rel_bias, l2_w_o, l2_ffn_norm_g, l2_ffn_w_up, l2_ffn_conv_w, l2_ffn_conv_b, l2_ffn_w_down, l3_ada_w, l3_ada_b, l3_attn_norm_g, l3_w_dq, l3_q_norm_g, l3_w_uq, l3_w_dkv, l3_kv_norm_g, l3_w_ukv, l3_w_o, l3_ffn_norm_g, l3_ffn_w_up, l3_ffn_conv_w, l3_ffn_conv_b, l3_ffn_w_down, final_norm_g):
    batch, n_lat, d = x.shape
    n_ctx = ctx.shape[1]
    lat_rows = batch * n_lat
    t_rows = lat_rows + batch * n_ctx
    rows = n_lat // GRID_W
    tm = 512
    assert batch + 1 <= 8 and n_ctx == TQ and n_lat % tm == 0 and (batch * n_ctx) % tm == 0
    assert n_lat & (n_lat - 1) == 0 and rows >= NA_KEY_ROWS and n_lat >= TQ + 2 * WINDOW

    def seg(i, tile):
        return jnp.minimum(i * tile // n_lat, batch)

    def tab_idx(i, tile):
        return jnp.where(i < lat_rows // tile, i % (n_lat // tile), n_lat // tile)

    cos_t, sin_t = _rope_tables(n_lat, tm)
    tabs128 = (cos_t, sin_t, tab_idx)
    tabs64 = _rope64_tables(n_lat, tm) + (tab_idx,)
    ones_g = jnp.ones((HEAD_DIM,), F32)

    xa = jnp.concatenate([x.reshape(lat_rows, d), ctx.reshape(batch * n_ctx, d)], axis=0)
    c8 = jnp.zeros((8, d), F32).at[:batch].set(c).at[batch].set(c_ctx)

    ada = [(l0_ada_w, l0_ada_b), (l1_ada_w, l1_ada_b), (l2_ada_w, l2_ada_b), (l3_ada_w, l3_ada_b)]
    attn_g = [l0_attn_norm_g, l1_attn_norm_g, l2_attn_norm_g, l3_attn_norm_g]
    ffn_g = [l0_ffn_norm_g, l1_ffn_norm_g, l2_ffn_norm_g, l3_ffn_norm_g]
    w_o = [l0_w_o, l1_w_o, l2_w_o, l3_w_o]
    ffn_p = [(l0_ffn_w_up, l0_ffn_conv_w, l0_ffn_conv_b, l0_ffn_w_down),
             (l1_ffn_w_up, l1_ffn_conv_w, l1_ffn_conv_b, l1_ffn_w_down),
             (l2_ffn_w_up, l2_ffn_conv_w, l2_ffn_conv_b, l2_ffn_w_down),
             (l3_ffn_w_up, l3_ffn_conv_w, l3_ffn_conv_b, l3_ffn_w_down)]
    w_qkv = [l0_w_qkv, l1_w_qkv, l2_w_qkv]
    gqa_scale = HEAD_DIM ** -0.5
    common = dict(batch=batch, n_lat=n_lat, n_ctx=n_ctx)
    gqa = dict(groups=N_KV_HEADS, r_heads=Q_PER_KV, q_base=0, k_base=N_HEADS, v_base=N_HEADS + N_KV_HEADS,
               **common)
    depth = 4

    for layer in range(depth):
        need_ctx = layer < depth - 1
        mod = _ada(c8, *ada[layer]).reshape(8, 6, d)
        h = _normmod(xa, attn_g[layer], mod, seg, rows=t_rows, shift_idx=0, scale_idx=1)

        if layer < 3:
            w = w_qkv[layer].astype(BF16)
            if layer == 0:
                qkv = _qkv_proj(h, w, tabs128, l0_q_norm_g, l0_k_norm_g, use_norm=True, use_rope=True,
                                scale=gqa_scale)
            elif layer == 1:
                qkv = _qkv_proj(h, w, tabs128, ones_g, ones_g, use_norm=False, use_rope=True, scale=gqa_scale)
            else:
                qkv = _qkv_proj(h, w, tabs128, ones_g, ones_g, use_norm=False, use_rope=False, scale=gqa_scale)
            sink = l1_sink if layer == 1 else None
            if layer == 0:
                o = _attention(qkv, qkv, qkv, None, mode="dense", **gqa)
            elif layer == 1:
                o = _attention(qkv, qkv, qkv, None, mode="window", sink=sink, bias=jnp.asarray(_window_bias()),
                               **gqa)
            else:
                o = _attention(qkv, qkv, qkv, None, mode="na", bias=_na_bias(l2_rel_bias, rows), **gqa)
            o = _attention(qkv, qkv, qkv, o, mode="ctx", sink=sink, **gqa)
        else:
            mla_scale = (MLA_NOPE + MLA_ROPE) ** -0.5
            rq = l3_w_uq.shape[0]
            w_uq = l3_w_uq.reshape(rq, N_HEADS, MLA_NOPE + MLA_ROPE)
            w_uq = jnp.pad(w_uq, ((0, 0), (0, 0), (0, MLA_QK - MLA_NOPE - MLA_ROPE))).reshape(rq, N_HEADS * MLA_QK)
            rank = l3_kv_norm_g.shape[0]
            w_dkv = jnp.pad(l3_w_dkv, ((0, 0), (0, LANES - MLA_ROPE)))
            cq = _mla_dq(h, l3_w_dq.astype(BF16), l3_q_norm_g)
            qh = _mla_uq(cq, w_uq.astype(BF16), tabs64, scale=mla_scale)
            ckv, kpe = _mla_dkv(h, w_dkv.astype(BF16), l3_kv_norm_g, tabs64, rank=rank)
            kh, vh = _mla_ukv(ckv, l3_w_ukv.astype(BF16), kpe)
            o = _attention(qh, kh, vh, None, mode="dense", groups=N_HEADS, r_heads=1, q_base=0, k_base=0,
                           v_base=0, tq=4 * TQ, **common)

        out_rows = t_rows if need_ctx else lat_rows
        xa = _proj_res(o, w_o[layer].astype(BF16), xa, mod, seg, rows=out_rows, gate_idx=2)
        h2 = _normmod(xa, ffn_g[layer], mod, seg, rows=out_rows, shift_idx=3, scale_idx=4)
        w_up, conv_w, conv_b, w_down = ffn_p[layer]
        u = _proj_plain(h2, w_up.astype(BF16), rows=out_rows)
        xa = _ffn_down(u, conv_w, conv_b, w_down.astype(BF16), xa, mod, seg, rows=out_rows, n_lat=n_lat,
                       n_ctx=n_ctx, lat_rows=lat_rows, gate_idx=5)

    return _final_norm(xa, final_norm_g, rows=lat_rows).reshape(batch, n_lat, d)
```

```python
import functools
import math

import jax
import jax.numpy as jnp
import numpy as np
from jax import lax
from jax.experimental import pallas as pl
from jax.experimental.pallas import tpu as pltpu

F32 = jnp.float32
BF16 = jnp.bfloat16

GRID_W = 64
HEAD_DIM = 128
N_HEADS = 32
N_KV_HEADS = 8
Q_PER_KV = 4
WINDOW = 128
NA_KH = 8
NA_KW = 16
NA_KEY_ROWS = 12
MLA_NOPE = 128
MLA_ROPE = 64
MLA_V = 128
MLA_QK = 256
ROPE_THETA = 10000.0
EPS = 1e-6
NEG_INF = -1e30
LANES = 128
TQ = 256
KV_UNROLL = 8
TM_BIG = 1024
TM = 512
TM_NORM = 256
TN = 512
MIB = 1024 * 1024


def _cparams(n_axes, vmem_mib):
    return pltpu.CompilerParams(dimension_semantics=("arbitrary",) * n_axes,
                                vmem_limit_bytes=vmem_mib * MIB)


def _dot(a, b):
    return jnp.dot(a, b, preferred_element_type=F32)


def _dot_nt(a, b):
    return lax.dot_general(a, b, (((1,), (1,)), ((), ())), preferred_element_type=F32)


def _rms(y, g):
    return y * lax.rsqrt(jnp.mean(y * y, axis=-1, keepdims=True) + EPS) * g


def _ada_kernel(c_ref, w_ref, b_ref, o_ref):
    x = c_ref[...]
    s = x / (1.0 + jnp.exp(-x))
    o_ref[...] = _dot(s.astype(BF16), w_ref[...].astype(BF16)) + b_ref[...]


def _ada(c8, w, b, tn=TN):
    d, n = w.shape
    return pl.pallas_call(
        _ada_kernel,
        out_shape=jax.ShapeDtypeStruct((8, n), F32),
        grid=(n // tn,),
        in_specs=[pl.BlockSpec((8, d), lambda j: (0, 0)),
                  pl.BlockSpec((d, tn), lambda j: (0, j)),
                  pl.BlockSpec((1, tn), lambda j: (0, j))],
        out_specs=pl.BlockSpec((8, tn), lambda j: (0, j)),
        compiler_params=_cparams(1, 40),
        name="ada",
    )(c8, w, b.reshape(1, n))


def _normmod_kernel(x_ref, g_ref, mod_ref, o_ref, *, shift_idx, scale_idx):
    y = _rms(x_ref[...], g_ref[...])
    y = y * (1.0 + mod_ref[0, scale_idx:scale_idx + 1, :]) + mod_ref[0, shift_idx:shift_idx + 1, :]
    o_ref[...] = y.astype(o_ref.dtype)


def _normmod(x, g, mod, seg, *, rows, shift_idx, scale_idx, tm=TM_NORM):
    d = x.shape[1]
    return pl.pallas_call(
        functools.partial(_normmod_kernel, shift_idx=shift_idx, scale_idx=scale_idx),
        out_shape=jax.ShapeDtypeStruct((rows, d), BF16),
        grid=(rows // tm,),
        in_specs=[pl.BlockSpec((tm, d), lambda i: (i, 0)),
                  pl.BlockSpec((1, d), lambda i: (0, 0)),
                  pl.BlockSpec((1, 6, d), lambda i: (seg(i, tm), 0, 0))],
        out_specs=pl.BlockSpec((tm, d), lambda i: (i, 0)),
        compiler_params=_cparams(1, 32),
        name="normmod",
    )(x, g.reshape(1, d), mod)


def _final_norm_kernel(x_ref, g_ref, o_ref):
    o_ref[...] = _rms(x_ref[...], g_ref[...])


def _final_norm(x, g, *, rows, tm=TM_NORM):
    d = x.shape[1]
    return pl.pallas_call(
        _final_norm_kernel,
        out_shape=jax.ShapeDtypeStruct((rows, d), F32),
        grid=(rows // tm,),
        in_specs=[pl.BlockSpec((tm, d), lambda i: (i, 0)),
                  pl.BlockSpec((1, d), lambda i: (0, 0))],
        out_specs=pl.BlockSpec((tm, d), lambda i: (i, 0)),
        compiler_params=_cparams(1, 32),
        name="final_norm",
    )(x, g.reshape(1, d))


def _rope128(y, cos, sin):
    return y * cos + pltpu.roll(y, HEAD_DIM // 2, 1) * sin


def _rope64(y, cos, s1, s2):
    half = MLA_ROPE // 2
    return y * cos + pltpu.roll(y, half, 1) * s1 + pltpu.roll(y, LANES - half, 1) * s2


def _cast_weight_tile(w_ref, wb_scr):
    @pl.when(pl.program_id(1) == 0)
    def _():
        wb_scr[...] = w_ref[...].astype(BF16)


def _qkv_kernel(a_ref, w_ref, cos_ref, sin_ref, gq_ref, gk_ref, o_ref, wb_scr, *,
                nq_tiles, nqk_tiles, hpt, use_norm, use_rope, scale):
    j = pl.program_id(0)
    _cast_weight_tile(w_ref, wb_scr)
    acc = _dot(a_ref[...], wb_scr[...])

    @pl.when(j < nqk_tiles)
    def _():
        is_q = j < nq_tiles
        fac = jnp.where(is_q, scale, 1.0).astype(F32)
        g = jnp.where(is_q, gq_ref[...], gk_ref[...])
        for r in range(hpt):
            y = acc[:, r * HEAD_DIM:(r + 1) * HEAD_DIM]
            if use_norm:
                y = _rms(y, g)
            if use_rope:
                y = _rope128(y, cos_ref[...], sin_ref[...])
            o_ref[r] = (y * fac).astype(BF16)

    @pl.when(j >= nqk_tiles)
    def _():
        for r in range(hpt):
            o_ref[r] = acc[:, r * HEAD_DIM:(r + 1) * HEAD_DIM].astype(BF16)


def _qkv_proj(h, w, tabs, gq, gk, *, use_norm, use_rope, scale, tm=TM_BIG, tn=TN):
    t, d = h.shape
    n = w.shape[1]
    hpt = tn // HEAD_DIM
    cos, sin, tab_idx = tabs
    kern = functools.partial(
        _qkv_kernel, nq_tiles=N_HEADS * HEAD_DIM // tn, nqk_tiles=(N_HEADS + N_KV_HEADS) * HEAD_DIM // tn,
        hpt=hpt, use_norm=use_norm, use_rope=use_rope, scale=scale)
    return pl.pallas_call(
        kern,
        out_shape=jax.ShapeDtypeStruct((n // HEAD_DIM, t, HEAD_DIM), BF16),
        grid=(n // tn, t // tm),
        in_specs=[pl.BlockSpec((tm, d), lambda j, i: (i, 0)),
                  pl.BlockSpec((d, tn), lambda j, i: (0, j)),
                  pl.BlockSpec((tm, HEAD_DIM), lambda j, i: (tab_idx(i, tm), 0)),
                  pl.BlockSpec((tm, HEAD_DIM), lambda j, i: (tab_idx(i, tm), 0)),
                  pl.BlockSpec((1, HEAD_DIM), lambda j, i: (0, 0)),
                  pl.BlockSpec((1, HEAD_DIM), lambda j, i: (0, 0))],
        out_specs=pl.BlockSpec((hpt, tm, HEAD_DIM), lambda j, i: (j, i, 0)),
        scratch_shapes=[pltpu.VMEM((d, tn), BF16)],
        compiler_params=_cparams(2, 48),
        name="qkv",
    )(h, w, cos, sin, gq.reshape(1, HEAD_DIM), gk.reshape(1, HEAD_DIM))


def _res_kernel(a_ref, w_ref, x_ref, mod_ref, o_ref, wb_scr, *, gate_idx):
    _cast_weight_tile(w_ref, wb_scr)
    acc = _dot(a_ref[...], wb_scr[...])
    o_ref[...] = x_ref[...] + mod_ref[0, gate_idx:gate_idx + 1, :] * acc


def _proj_res(a, w, xres, mod, seg, *, rows, gate_idx, tm=TM_BIG, tn=TN):
    k, n = w.shape
    return pl.pallas_call(
        functools.partial(_res_kernel, gate_idx=gate_idx),
        out_shape=jax.ShapeDtypeStruct((rows, n), F32),
        grid=(n // tn, rows // tm),
        in_specs=[pl.BlockSpec((tm, k), lambda j, i: (i, 0)),
                  pl.BlockSpec((k, tn), lambda j, i: (0, j)),
                  pl.BlockSpec((tm, tn), lambda j, i: (i, j)),
                  pl.BlockSpec((1, 6, tn), lambda j, i: (seg(i, tm), 0, j))],
        out_specs=pl.BlockSpec((tm, tn), lambda j, i: (i, j)),
        scratch_shapes=[pltpu.VMEM((k, tn), BF16)],
        compiler_params=_cparams(2, 48),
        name="proj_res",
    )(a, w, xres, mod)


def _plain_kernel(a_ref, w_ref, o_ref, wb_scr):
    _cast_weight_tile(w_ref, wb_scr)
    o_ref[...] = _dot(a_ref[...], wb_scr[...]).astype(o_ref.dtype)


def _proj_plain(a, w, *, rows, tm=TM_BIG, tn=TN):
    k, n = w.shape
    return pl.pallas_call(
        _plain_kernel,
        out_shape=jax.ShapeDtypeStruct((rows, n), BF16),
        grid=(n // tn, rows // tm),
        in_specs=[pl.BlockSpec((tm, k), lambda j, i: (i, 0)),
                  pl.BlockSpec((k, tn), lambda j, i: (0, j))],
        out_specs=pl.BlockSpec((tm, tn), lambda j, i: (i, j)),
        scratch_shapes=[pltpu.VMEM((k, tn), BF16)],
        compiler_params=_cparams(2, 48),
        name="proj_plain",
    )(a, w)


def _ffn_down_kernel(a_ref, v_ref, ap_ref, an_ref, cw_ref, cb_ref, w_ref, x_ref, mod_ref, o_ref, g0_scr, g1_scr, *,
                     tm, n_tiles, sub_chunks, n_lat, n_ctx, lat_rows, gate_idx):
    r = pl.program_id(0)
    j = pl.program_id(1)

    @pl.when((r == 0) & (j == 0))
    def _():
        g1_scr[...] = jnp.zeros(g1_scr.shape, BF16)

    tile = jnp.minimum(r, n_tiles - 1)
    row = lax.broadcasted_iota(jnp.int32, (tm, LANES), 0)
    grow = tile * tm + row
    is_ctx = grow >= lat_rows
    first = ((grow & (n_ctx - 1)) == 0) & (is_ctx | ((grow & (n_lat - 1)) == 0))
    nxt = grow + 1
    last = ((nxt & (n_ctx - 1)) == 0) & (is_ctx | ((nxt & (n_lat - 1)) == 0))
    top = row == 0
    bot = row == tm - 1

    def step(g_write, g_read):
        for c in range(sub_chunks):
            sl = pl.ds(pl.multiple_of((j * sub_chunks + c) * LANES, LANES), LANES)
            a = a_ref[:, sl].astype(F32)
            a_dn = jnp.where(top, ap_ref[7:8, sl].astype(F32), pltpu.roll(a, 1, 0))
            a_dn = jnp.where(first, 0.0, a_dn)
            a_up = jnp.where(bot, an_ref[0:1, sl].astype(F32), pltpu.roll(a, tm - 1, 0))
            a_up = jnp.where(last, 0.0, a_up)
            cv = cw_ref[0:1, sl] * a_dn + cw_ref[1:2, sl] * a + cw_ref[2:3, sl] * a_up + cb_ref[:, sl]
            g_write[:, sl] = (cv / (1.0 + jnp.exp(-cv)) * v_ref[:, sl].astype(F32)).astype(BF16)
        acc = _dot(g_read[...], w_ref[...])
        o_ref[...] = x_ref[...] + mod_ref[0, gate_idx:gate_idx + 1, :] * acc

    pl.when(r % 2 == 0)(lambda: step(g0_scr, g1_scr))
    pl.when(r % 2 == 1)(lambda: step(g1_scr, g0_scr))


def _ffn_down(u, conv_w, conv_b, w, xres, mod, seg, *, rows, n_lat, n_ctx, lat_rows, gate_idx, tm=TM, tn=TN):
    dff, n = w.shape
    nb8 = u.shape[0] // 8
    n_tiles = rows // tm
    n_j = n // tn
    sub_chunks = dff // n_j // LANES
    assert sub_chunks * n_j * LANES == dff
    kern = functools.partial(_ffn_down_kernel, tm=tm, n_tiles=n_tiles, sub_chunks=sub_chunks, n_lat=n_lat,
                             n_ctx=n_ctx, lat_rows=lat_rows, gate_idx=gate_idx)

    def cur(r):
        return jnp.minimum(r, n_tiles - 1)

    def prev(r):
        return jnp.maximum(r - 1, 0)

    return pl.pallas_call(
        kern,
        out_shape=jax.ShapeDtypeStruct((rows, n), F32),
        grid=(n_tiles + 1, n_j),
        in_specs=[pl.BlockSpec((tm, dff), lambda r, j: (cur(r), 0)),
                  pl.BlockSpec((tm, dff), lambda r, j: (cur(r), 1)),
                  pl.BlockSpec((8, dff), lambda r, j: (jnp.maximum(cur(r) * (tm // 8) - 1, 0), 0)),
                  pl.BlockSpec((8, dff), lambda r, j: (jnp.minimum((cur(r) + 1) * (tm // 8), nb8 - 1), 0)),
                  pl.BlockSpec((3, dff), lambda r, j: (0, 0)),
                  pl.BlockSpec((1, dff), lambda r, j: (0, 0)),
                  pl.BlockSpec((dff, tn), lambda r, j: (0, j)),
                  pl.BlockSpec((tm, tn), lambda r, j: (prev(r), j)),
                  pl.BlockSpec((1, 6, tn), lambda r, j: (seg(prev(r), tm), 0, j))],
        out_specs=pl.BlockSpec((tm, tn), lambda r, j: (prev(r), jnp.where(r == 0, 0, j))),
        scratch_shapes=[pltpu.VMEM((tm, dff), BF16), pltpu.VMEM((tm, dff), BF16)],
        compiler_params=_cparams(2, 56),
        name="ffn_down",
    )(u, u, u, u, conv_w, conv_b.reshape(1, dff), w, xres, mod)


def _softmax_first(s, v, sink_tile):
    n = s.shape[1]
    if sink_tile is not None:
        s = jnp.concatenate([s, sink_tile], axis=1)
    m = jnp.max(s, axis=1, keepdims=True)
    p = jnp.exp(s - m)
    return m, jnp.sum(p, axis=1, keepdims=True), _dot(p[:, :n].astype(BF16), v)


def _softmax_empty(rows, dv):
    return jnp.full((rows, 1), NEG_INF, F32), jnp.zeros((rows, 1), F32), jnp.zeros((rows, dv), F32)


def _softmax_update(carry, s, v, sink_tile=None):
    m, l, acc = carry
    n = s.shape[1]
    if sink_tile is not None:
        s = jnp.concatenate([s, sink_tile], axis=1)
    m_new = jnp.maximum(m, jnp.max(s, axis=1, keepdims=True))
    p = jnp.exp(s - m_new)
    alpha = jnp.exp(m - m_new)
    return m_new, alpha * l + jnp.sum(p, axis=1, keepdims=True), alpha * acc + _dot(p[:, :n].astype(BF16), v)


def _attn_kernel(*refs, mode, r_heads, n_lat, tk, has_sink, with_ctx, rows, nt):
    refs = list(refs)
    q_ref, kc_ref, vc_ref, kl_ref, vl_ref = refs[:5]
    pos = 5
    sink_ref = bias_ref = None
    if has_sink:
        sink_ref = refs[pos]
        pos += 1
    if mode in ("window", "na"):
        bias_ref = refs[pos]
        pos += 1
    o_ref = refs[-1]
    t = pl.program_id(2)
    tq = q_ref.shape[1]
    dq = q_ref.shape[2]
    m_rows = r_heads * tq

    def prologue():
        q = q_ref[...].reshape(m_rows, dq)
        sink_tile = None
        if has_sink:
            sink_tile = jnp.concatenate([jnp.broadcast_to(sink_ref[r], (tq, LANES)) for r in range(r_heads)], axis=0)
        return q, sink_tile, _dot_nt(q, kc_ref[0])

    def finish(carry):
        _, l, acc = carry
        o = acc / l
        dv = acc.shape[-1]
        for r in range(r_heads):
            o_ref[:, r * dv:(r + 1) * dv] = o[r * tq:(r + 1) * tq].astype(o_ref.dtype)

    def latent_tile():
        q, sink_tile, s_ctx = prologue()
        if mode == "dense":
            n_chunks = n_lat // tk
            unroll = math.gcd(KV_UNROLL, n_chunks)

            def chunk(c):
                return pl.ds(pl.multiple_of(c * tk, tk), tk)

            carry = _softmax_first(s_ctx, vc_ref[0], sink_tile)

            def body(c, carry):
                scores = [_dot_nt(q, kl_ref[0, chunk(unroll * c + u), :]) for u in range(unroll)]
                for u in range(unroll):
                    carry = _softmax_update(carry, scores[u], vl_ref[0, chunk(unroll * c + u), :])
                return carry

            carry = lax.fori_loop(0, n_chunks // unroll, body, carry)
        else:
            variant = jnp.where(t == 0, 0, jnp.where(t == nt - 1, 2, 1))
            if mode == "window":
                span = tq + 2 * WINDOW
                start = pl.multiple_of(jnp.clip(t * tq - WINDOW, 0, n_lat - span), WINDOW)
                bias = jnp.concatenate([bias_ref[variant]] * r_heads, axis=0)
            else:
                span = NA_KEY_ROWS * GRID_W
                ks_row = jnp.clip(t * (tq // GRID_W) - NA_KH // 2, 0, rows - NA_KEY_ROWS)
                start = pl.multiple_of(ks_row * GRID_W, GRID_W)
                bias = bias_ref[variant].reshape(m_rows, span)
            sl = pl.ds(start, span)
            s_loc = _dot_nt(q, kl_ref[0, sl, :]) + bias
            s_all = jnp.concatenate([s_ctx, s_loc], axis=1)
            v_all = jnp.concatenate([vc_ref[0], vl_ref[0, sl, :]], axis=0)
            carry = _softmax_update(_softmax_empty(m_rows, v_all.shape[1]), s_all, v_all, sink_tile)
        finish(carry)

    def context_tile():
        _, sink_tile, s_ctx = prologue()
        finish(_softmax_first(s_ctx, vc_ref[0], sink_tile))

    if with_ctx:
        pl.when(t < nt)(latent_tile)
        pl.when(t == nt)(context_tile)
    else:
        latent_tile()


def _window_bias():
    span = TQ + 2 * WINDOW
    d = np.arange(TQ)[:, None] - np.arange(span)[None, :]
    offsets = (0, WINDOW, span - TQ)
    return np.stack([np.where(np.abs(d + o) <= WINDOW, 0.0, NEG_INF) for o in offsets]).astype(np.float32)


def _attention(qa, ka, va, *, mode, with_ctx, batch, n_lat, n_ctx, groups, r_heads, q_base, k_base, v_base,
               sink=None, bias=None, tq=TQ, tk=512):
    dq = qa.shape[2]
    dv = va.shape[2]
    lat_rows = batch * n_lat
    nq = n_lat // tq
    ctx_blk = lat_rows // n_ctx
    assert not with_ctx or tq == n_ctx

    def qrow(b, g, t):
        return jnp.where(t < nq, b * nq + t, lat_rows // tq + b) if with_ctx else b * nq + t

    in_specs = [pl.BlockSpec((r_heads, tq, dq), lambda b, g, t: (q_base + g, qrow(b, g, t), 0)),
                pl.BlockSpec((1, n_ctx, dq), lambda b, g, t: (k_base + g, ctx_blk + b, 0)),
                pl.BlockSpec((1, n_ctx, dv), lambda b, g, t: (v_base + g, ctx_blk + b, 0)),
                pl.BlockSpec((1, n_lat, dq), lambda b, g, t: (k_base + g, b, 0)),
                pl.BlockSpec((1, n_lat, dv), lambda b, g, t: (v_base + g, b, 0))]
    args = [qa, ka, va, ka, va]
    if sink is not None:
        in_specs.append(pl.BlockSpec((r_heads, 1, LANES), lambda b, g, t: (g, 0, 0)))
        args.append(jnp.full((sink.shape[0], 1, LANES), NEG_INF, F32).at[:, 0, 0].set(sink))
    if mode == "window":
        in_specs.append(pl.BlockSpec(bias.shape, lambda b, g, t: (0, 0, 0)))
        args.append(bias)
    if mode == "na":
        in_specs.append(pl.BlockSpec((3, r_heads, tq, NA_KEY_ROWS * GRID_W), lambda b, g, t: (0, g, 0, 0)))
        args.append(bias)
    kern = functools.partial(_attn_kernel, mode=mode, r_heads=r_heads, n_lat=n_lat, tk=tk, has_sink=sink is not None,
                             with_ctx=with_ctx, rows=n_lat // GRID_W, nt=nq)
    out_rows = lat_rows + (batch * n_ctx if with_ctx else 0)
    return pl.pallas_call(
        kern,
        out_shape=jax.ShapeDtypeStruct((out_rows, groups * r_heads * dv), BF16),
        grid=(batch, groups, nq + (1 if with_ctx else 0)),
        in_specs=in_specs,
        out_specs=pl.BlockSpec((tq, r_heads * dv), lambda b, g, t: (qrow(b, g, t), g)),
        compiler_params=_cparams(3, 48),
        name="attn_" + mode,
    )(*args)


def _na_variants(rows, q_rows):
    nt = rows // q_rows

    def structure(t):
        ks_row = int(np.clip(t * q_rows - NA_KH // 2, 0, rows - NA_KEY_ROWS))
        out = []
        for qr in range(q_rows):
            r = t * q_rows + qr
            rs = int(np.clip(r - NA_KH // 2, 0, rows - NA_KH))
            out.append([(rs <= ks_row + j < rs + NA_KH, ks_row + j - r + NA_KH - 1) for j in range(NA_KEY_ROWS)])
        return out

    reps = [structure(0), structure(1), structure(nt - 1)]
    for t in range(nt):
        want = reps[0] if t == 0 else reps[2] if t == nt - 1 else reps[1]
        got = structure(t)
        for wq, gq in zip(want, got):
            for (wv, wa), (gv, ga) in zip(wq, gq):
                assert wv == gv and (not wv or wa == ga), "neighbourhood tile variants do not cover this grid"
    return reps


def _na_bias_kernel(rb_ref, o_ref, *, variants):
    c = lax.broadcasted_iota(jnp.int32, (GRID_W, LANES), 0)
    lane = lax.broadcasted_iota(jnp.int32, (GRID_W, LANES), 1)
    kc = lane & (GRID_W - 1)
    cs = jnp.clip(c - NA_KW // 2, 0, GRID_W - NA_KW)
    col_ok = (kc >= cs) & (kc < cs + NA_KW)
    low = lane < GRID_W
    neg = jnp.full((GRID_W, LANES), NEG_INF, F32)
    for v, var in enumerate(variants):
        for qr, krows in enumerate(var):
            for jp in range(NA_KEY_ROWS // 2):
                halves = []
                for hf in range(2):
                    valid, a = krows[2 * jp + hf]
                    if valid:
                        x = jnp.broadcast_to(rb_ref[0, a:a + 1, :], (GRID_W, LANES))
                        shift = (LANES - (NA_KW - 1) + GRID_W * hf) % LANES
                        halves.append(pltpu.roll(x, shift, 1, stride=1, stride_axis=0))
                    else:
                        halves.append(neg)
                tile = jnp.where(col_ok, jnp.where(low, halves[0], halves[1]), NEG_INF)
                o_ref[v, 0, qr * GRID_W:(qr + 1) * GRID_W, jp * LANES:(jp + 1) * LANES] = tile


def _na_bias(rel_bias, rows):
    q_rows = TQ // GRID_W
    variants = _na_variants(rows, q_rows)
    nh, na, nb = rel_bias.shape
    rb = jnp.zeros((nh, 16, LANES), F32).at[:, :na, :nb].set(rel_bias)
    span = NA_KEY_ROWS * GRID_W
    return pl.pallas_call(
        functools.partial(_na_bias_kernel, variants=variants),
        out_shape=jax.ShapeDtypeStruct((3, nh, TQ, span), F32),
        grid=(nh,),
        in_specs=[pl.BlockSpec((1, 16, LANES), lambda h: (h, 0, 0))],
        out_specs=pl.BlockSpec((3, 1, TQ, span), lambda h: (0, h, 0, 0)),
        compiler_params=_cparams(1, 32),
        name="na_bias",
    )(rb)


def _mla_dq_kernel(a_ref, w_ref, g_ref, o_ref):
    o_ref[...] = _rms(_dot(a_ref[...], w_ref[...]), g_ref[...]).astype(BF16)


def _mla_dq(h, w, g, tm=TM):
    t, d = h.shape
    n = w.shape[1]
    return pl.pallas_call(
        _mla_dq_kernel,
        out_shape=jax.ShapeDtypeStruct((t, n), BF16),
        grid=(t // tm,),
        in_specs=[pl.BlockSpec((tm, d), lambda i: (i, 0)),
                  pl.BlockSpec((d, n), lambda i: (0, 0)),
                  pl.BlockSpec((1, n), lambda i: (0, 0))],
        out_specs=pl.BlockSpec((tm, n), lambda i: (i, 0)),
        compiler_params=_cparams(1, 48),
        name="mla_dq",
    )(h, w, g.reshape(1, n))


def _mla_uq_kernel(a_ref, w_ref, cos_ref, s1_ref, s2_ref, o_ref, *, hpt, scale):
    acc = _dot(a_ref[...], w_ref[...])
    for r in range(hpt):
        base = r * MLA_QK
        o_ref[r, :, :MLA_NOPE] = (acc[:, base:base + MLA_NOPE] * scale).astype(BF16)
        pe = _rope64(acc[:, base + MLA_NOPE:base + MLA_QK], cos_ref[...], s1_ref[...], s2_ref[...])
        o_ref[r, :, MLA_NOPE:] = (pe * scale).astype(BF16)


def _mla_uq(cq, w, tabs, *, scale, tm=TM, tn=TN):
    t, k = cq.shape
    n = w.shape[1]
    hpt = tn // MLA_QK
    cos, s1, s2, tab_idx = tabs
    tab_spec = pl.BlockSpec((tm, LANES), lambda i, j: (tab_idx(i, tm), 0))
    return pl.pallas_call(
        functools.partial(_mla_uq_kernel, hpt=hpt, scale=scale),
        out_shape=jax.ShapeDtypeStruct((n // MLA_QK, t, MLA_QK), BF16),
        grid=(t // tm, n // tn),
        in_specs=[pl.BlockSpec((tm, k), lambda i, j: (i, 0)),
                  pl.BlockSpec((k, tn), lambda i, j: (0, j)),
                  tab_spec, tab_spec, tab_spec],
        out_specs=pl.BlockSpec((hpt, tm, MLA_QK), lambda i, j: (j, i, 0)),
        compiler_params=_cparams(2, 40),
        name="mla_uq",
    )(cq, w, cos, s1, s2)


def _mla_dkv_kernel(a_ref, w_ref, g_ref, cos_ref, s1_ref, s2_ref, ckv_ref, kpe_ref, *, rank):
    acc = _dot(a_ref[...], w_ref[...])
    ckv_ref[...] = _rms(acc[:, :rank], g_ref[...]).astype(BF16)
    kpe_ref[...] = _rope64(acc[:, rank:], cos_ref[...], s1_ref[...], s2_ref[...]).astype(BF16)


def _mla_dkv(h, w, g, tabs, *, rank, tm=TM):
    t, d = h.shape
    n = w.shape[1]
    cos, s1, s2, tab_idx = tabs
    tab_spec = pl.BlockSpec((tm, LANES), lambda i: (tab_idx(i, tm), 0))
    return pl.pallas_call(
        functools.partial(_mla_dkv_kernel, rank=rank),
        out_shape=(jax.ShapeDtypeStruct((t, rank), BF16), jax.ShapeDtypeStruct((t, LANES), BF16)),
        grid=(t // tm,),
        in_specs=[pl.BlockSpec((tm, d), lambda i: (i, 0)),
                  pl.BlockSpec((d, n), lambda i: (0, 0)),
                  pl.BlockSpec((1, rank), lambda i: (0, 0)),
                  tab_spec, tab_spec, tab_spec],
        out_specs=(pl.BlockSpec((tm, rank), lambda i: (i, 0)),
                   pl.BlockSpec((tm, LANES), lambda i: (i, 0))),
        compiler_params=_cparams(1, 40),
        name="mla_dkv",
    )(h, w, g.reshape(1, rank), cos, s1, s2)


def _mla_ukv_kernel(a_ref, w_ref, kpe_ref, k_ref, v_ref, *, hpt):
    acc = _dot(a_ref[...], w_ref[...])
    for r in range(hpt):
        base = r * (MLA_NOPE + MLA_V)
        k_ref[r, :, :MLA_NOPE] = acc[:, base:base + MLA_NOPE].astype(BF16)
        k_ref[r, :, MLA_NOPE:] = kpe_ref[...]
        v_ref[r] = acc[:, base + MLA_NOPE:base + MLA_NOPE + MLA_V].astype(BF16)


def _mla_ukv(ckv, w, kpe, *, tm=TM, tn=TN):
    t, k = ckv.shape
    n = w.shape[1]
    hpt = tn // (MLA_NOPE + MLA_V)
    nh = n // (MLA_NOPE + MLA_V)
    return pl.pallas_call(
        functools.partial(_mla_ukv_kernel, hpt=hpt),
        out_shape=(jax.ShapeDtypeStruct((nh, t, MLA_QK), BF16), jax.ShapeDtypeStruct((nh, t, MLA_V), BF16)),
        grid=(t // tm, n // tn),
        in_specs=[pl.BlockSpec((tm, k), lambda i, j: (i, 0)),
                  pl.BlockSpec((k, tn), lambda i, j: (0, j)),
                  pl.BlockSpec((tm, LANES), lambda i, j: (i, 0))],
        out_specs=(pl.BlockSpec((hpt, tm, MLA_QK), lambda i, j: (j, i, 0)),
                   pl.BlockSpec((hpt, tm, MLA_V), lambda i, j: (j, i, 0))),
        compiler_params=_cparams(2, 40),
        name="mla_ukv",
    )(ckv, w, kpe)


def _axial_angles(n_tok, rot_dim):
    n_freq = rot_dim // 4
    freqs = ROPE_THETA ** (-jnp.arange(n_freq, dtype=F32) / n_freq)
    t = jnp.arange(n_tok)
    row = (t // GRID_W).astype(F32)
    col = (t % GRID_W).astype(F32)
    return jnp.concatenate([row[:, None] * freqs, col[:, None] * freqs], axis=-1)


def _rope_tables(n_lat, tail):
    ang = _axial_angles(n_lat, HEAD_DIM)
    cos, sin = jnp.cos(ang), jnp.sin(ang)
    cos_t = jnp.concatenate([cos, cos], axis=-1)
    sin_t = jnp.concatenate([-sin, sin], axis=-1)
    cos_t = jnp.concatenate([cos_t, jnp.ones((tail, HEAD_DIM), F32)], axis=0)
    sin_t = jnp.concatenate([sin_t, jnp.zeros((tail, HEAD_DIM), F32)], axis=0)
    return cos_t, sin_t


def _rope64_tables(n_lat, tail):
    ang = _axial_angles(n_lat, MLA_ROPE)
    cos, sin = jnp.cos(ang), jnp.sin(ang)
    half = MLA_ROPE // 2
    z = jnp.zeros((n_lat, half), F32)
    zpad = jnp.zeros((n_lat, LANES - MLA_ROPE), F32)
    cos_t = jnp.concatenate([cos, cos, zpad], axis=-1)
    s1 = jnp.concatenate([z, sin, zpad], axis=-1)
    s2 = jnp.concatenate([-sin, z, zpad], axis=-1)
    ident = jnp.concatenate([jnp.ones((tail, MLA_ROPE), F32), jnp.zeros((tail, LANES - MLA_ROPE), F32)], axis=-1)
    zt = jnp.zeros((tail, LANES), F32)
    return (jnp.concatenate([cos_t, ident], axis=0), jnp.concatenate([s1, zt], axis=0),
            jnp.concatenate([s2, zt], axis=0))


def kernel(x, c, ctx, c_ctx, l0_ada_w, l0_ada_b, l0_attn_norm_g, l0_w_qkv, l0_q_norm_g, l0_k_norm_g, l0_w_o, l0_ffn_norm_g, l0_ffn_w_up, l0_ffn_conv_w, l0_ffn_conv_b, l0_ffn_w_down, l1_ada_w, l1_ada_b, l1_attn_norm_g, l1_w_qkv, l1_sink, l1_w_o, l1_ffn_norm_g, l1_ffn_w_up, l1_ffn_conv_w, l1_ffn_conv_b, l1_ffn_w_down, l2_ada_w, l2_ada_b, l2_attn_norm_g, l2_w_qkv, l2_rel_bias, l2_w_o, l2_ffn_norm_g, l2_ffn_w_up, l2_ffn_conv_w, l2_ffn_conv_b, l2_ffn_w_down, l3_ada_w, l3_ada_b, l3_attn_norm_g, l3_w_dq, l3_q_norm_g, l3_w_uq, l3_w_dkv, l3_kv_norm_g, l3_w_ukv, l3_w_o, l3_ffn_norm_g, l3_ffn_w_up, l3_ffn_conv_w, l3_ffn_conv_b, l3_ffn_w_down, final_norm_g):
    batch, n_lat, d = x.shape
    n_ctx = ctx.shape[1]
    lat_rows = batch * n_lat
    t_rows = lat_rows + batch * n_ctx
    rows = n_lat // GRID_W
    assert batch + 1 <= 8 and n_ctx == TQ and n_lat % TM_BIG == 0 and (batch * n_ctx) % TM_BIG == 0
    assert n_lat & (n_lat - 1) == 0 and rows >= NA_KEY_ROWS and n_lat >= TQ + 2 * WINDOW

    def seg(i, tile):
        return jnp.minimum(i * tile // n_lat, batch)

    def tab_idx(i, tile):
        return jnp.where(i < lat_rows // tile, i % (n_lat // tile), n_lat // tile)

    cos_t, sin_t = _rope_tables(n_lat, TM_BIG)
    tabs128 = (cos_t, sin_t, tab_idx)
    tabs64 = _rope64_tables(n_lat, TM_BIG) + (tab_idx,)
    ones_g = jnp.ones((HEAD_DIM,), F32)

    xa = jnp.concatenate([x.reshape(lat_rows, d), ctx.reshape(batch * n_ctx, d)], axis=0)
    c8 = jnp.zeros((8, d), F32).at[:batch].set(c).at[batch].set(c_ctx)

    ada = [(l0_ada_w, l0_ada_b), (l1_ada_w, l1_ada_b), (l2_ada_w, l2_ada_b), (l3_ada_w, l3_ada_b)]
    attn_g = [l0_attn_norm_g, l1_attn_norm_g, l2_attn_norm_g, l3_attn_norm_g]
    ffn_g = [l0_ffn_norm_g, l1_ffn_norm_g, l2_ffn_norm_g, l3_ffn_norm_g]
    w_o = [l0_w_o, l1_w_o, l2_w_o, l3_w_o]
    ffn_p = [(l0_ffn_w_up, l0_ffn_conv_w, l0_ffn_conv_b, l0_ffn_w_down),
             (l1_ffn_w_up, l1_ffn_conv_w, l1_ffn_conv_b, l1_ffn_w_down),
             (l2_ffn_w_up, l2_ffn_conv_w, l2_ffn_conv_b, l2_ffn_w_down),
             (l3_ffn_w_up, l3_ffn_conv_w, l3_ffn_conv_b, l3_ffn_w_down)]
    w_qkv = [l0_w_qkv, l1_w_qkv, l2_w_qkv]
    gqa_scale = HEAD_DIM ** -0.5
    common = dict(batch=batch, n_lat=n_lat, n_ctx=n_ctx)
    gqa = dict(groups=N_KV_HEADS, r_heads=Q_PER_KV, q_base=0, k_base=N_HEADS, v_base=N_HEADS + N_KV_HEADS,
               **common)
    depth = 4

    for layer in range(depth):
        need_ctx = layer < depth - 1
        mod = _ada(c8, *ada[layer]).reshape(8, 6, d)
        h = _normmod(xa, attn_g[layer], mod, seg, rows=t_rows, shift_idx=0, scale_idx=1)

        if layer < 3:
            w = w_qkv[layer]
            if layer == 0:
                qkv = _qkv_proj(h, w, tabs128, l0_q_norm_g, l0_k_norm_g, use_norm=True, use_rope=True,
                                scale=gqa_scale)
            elif layer == 1:
                qkv = _qkv_proj(h, w, tabs128, ones_g, ones_g, use_norm=False, use_rope=True, scale=gqa_scale)
            else:
                qkv = _qkv_proj(h, w, tabs128, ones_g, ones_g, use_norm=False, use_rope=False, scale=gqa_scale)
            if layer == 0:
                o = _attention(qkv, qkv, qkv, mode="dense", with_ctx=True, **gqa)
            elif layer == 1:
                o = _attention(qkv, qkv, qkv, mode="window", with_ctx=True, sink=l1_sink,
                               bias=jnp.asarray(_window_bias()), **gqa)
            else:
                o = _attention(qkv, qkv, qkv, mode="na", with_ctx=True, bias=_na_bias(l2_rel_bias, rows), **gqa)
        else:
            mla_scale = (MLA_NOPE + MLA_ROPE) ** -0.5
            rq = l3_w_uq.shape[0]
            w_uq = l3_w_uq.reshape(rq, N_HEADS, MLA_NOPE + MLA_ROPE)
            w_uq = jnp.pad(w_uq, ((0, 0), (0, 0), (0, MLA_QK - MLA_NOPE - MLA_ROPE))).reshape(rq, N_HEADS * MLA_QK)
            rank = l3_kv_norm_g.shape[0]
            w_dkv = jnp.pad(l3_w_dkv, ((0, 0), (0, LANES - MLA_ROPE)))
            cq = _mla_dq(h, l3_w_dq.astype(BF16), l3_q_norm_g)
            qh = _mla_uq(cq, w_uq.astype(BF16), tabs64, scale=mla_scale)
            ckv, kpe = _mla_dkv(h, w_dkv.astype(BF16), l3_kv_norm_g, tabs64, rank=rank)
            kh, vh = _mla_ukv(ckv, l3_w_ukv.astype(BF16), kpe)
            o = _attention(qh, kh, vh, mode="dense", with_ctx=False, groups=N_HEADS, r_heads=1, q_base=0, k_base=0,
                           v_base=0, tq=4 * TQ, **common)

        out_rows = t_rows if need_ctx else lat_rows
        xa = _proj_res(o, w_o[layer], xa, mod, seg, rows=out_rows, gate_idx=2)
        h2 = _normmod(xa, ffn_g[layer], mod, seg, rows=out_rows, shift_idx=3, scale_idx=4)
        w_up, conv_w, conv_b, w_down = ffn_p[layer]
        u = _proj_plain(h2, w_up, rows=out_rows)
        xa = _ffn_down(u, conv_w, conv_b, w_down.astype(BF16), xa, mod, seg, rows=out_rows, n_lat=n_lat,
                       n_ctx=n_ctx, lat_rows=lat_rows, gate_idx=5)

    return _final_norm(xa, final_norm_g, rows=lat_rows).reshape(batch, n_lat, d)
```

```python
import functools
import math

import jax
import jax.numpy as jnp
import numpy as np
from jax import lax
from jax.experimental import pallas as pl
from jax.experimental.pallas import tpu as pltpu

F32 = jnp.float32
BF16 = jnp.bfloat16

GRID_W = 64
HEAD_DIM = 128
N_HEADS = 32
N_KV_HEADS = 8
Q_PER_KV = 4
WINDOW = 128
NA_KH = 8
NA_KW = 16
NA_KEY_ROWS = 12
MLA_NOPE = 128
MLA_ROPE = 64
MLA_V = 128
MLA_QK = 256
ROPE_THETA = 10000.0
EPS = 1e-6
NEG_INF = -1e30
LANES = 128
TQ = 256
TK = 1024
KV_UNROLL = 8
TM_BIG = 1024
TM = 512
TM_NORM = 256
TN = 512
MIB = 1024 * 1024


def _cparams(n_axes, vmem_mib):
    return pltpu.CompilerParams(dimension_semantics=("arbitrary",) * n_axes,
                                vmem_limit_bytes=vmem_mib * MIB)


def _dot(a, b):
    return jnp.dot(a, b, preferred_element_type=F32)


def _dot_nt(a, b):
    return lax.dot_general(a, b, (((1,), (1,)), ((), ())), preferred_element_type=F32)


def _rms(y, g):
    return y * lax.rsqrt(jnp.mean(y * y, axis=-1, keepdims=True) + EPS) * g


def _ada_kernel(c_ref, w_ref, b_ref, o_ref):
    x = c_ref[...]
    s = x / (1.0 + jnp.exp(-x))
    o_ref[...] = _dot(s.astype(BF16), w_ref[...].astype(BF16)) + b_ref[...]


def _ada(c8, w, b, tn=TN):
    d, n = w.shape
    return pl.pallas_call(
        _ada_kernel,
        out_shape=jax.ShapeDtypeStruct((8, n), F32),
        grid=(n // tn,),
        in_specs=[pl.BlockSpec((8, d), lambda j: (0, 0)),
                  pl.BlockSpec((d, tn), lambda j: (0, j)),
                  pl.BlockSpec((1, tn), lambda j: (0, j))],
        out_specs=pl.BlockSpec((8, tn), lambda j: (0, j)),
        compiler_params=_cparams(1, 40),
        name="ada",
    )(c8, w, b.reshape(1, n))


def _normmod_kernel(x_ref, g_ref, mod_ref, o_ref, *, shift_idx, scale_idx):
    y = _rms(x_ref[...], g_ref[...])
    y = y * (1.0 + mod_ref[0, scale_idx:scale_idx + 1, :]) + mod_ref[0, shift_idx:shift_idx + 1, :]
    o_ref[...] = y.astype(o_ref.dtype)


def _normmod(x, g, mod, seg, *, rows, shift_idx, scale_idx, tm=TM_NORM):
    d = x.shape[1]
    return pl.pallas_call(
        functools.partial(_normmod_kernel, shift_idx=shift_idx, scale_idx=scale_idx),
        out_shape=jax.ShapeDtypeStruct((rows, d), BF16),
        grid=(rows // tm,),
        in_specs=[pl.BlockSpec((tm, d), lambda i: (i, 0)),
                  pl.BlockSpec((1, d), lambda i: (0, 0)),
                  pl.BlockSpec((1, 6, d), lambda i: (seg(i, tm), 0, 0))],
        out_specs=pl.BlockSpec((tm, d), lambda i: (i, 0)),
        compiler_params=_cparams(1, 32),
        name="normmod",
    )(x, g.reshape(1, d), mod)


def _final_norm_kernel(x_ref, g_ref, o_ref):
    o_ref[...] = _rms(x_ref[...], g_ref[...])


def _final_norm(x, g, *, rows, tm=TM_NORM):
    d = x.shape[1]
    return pl.pallas_call(
        _final_norm_kernel,
        out_shape=jax.ShapeDtypeStruct((rows, d), F32),
        grid=(rows // tm,),
        in_specs=[pl.BlockSpec((tm, d), lambda i: (i, 0)),
                  pl.BlockSpec((1, d), lambda i: (0, 0))],
        out_specs=pl.BlockSpec((tm, d), lambda i: (i, 0)),
        compiler_params=_cparams(1, 32),
        name="final_norm",
    )(x, g.reshape(1, d))


def _rope128(y, cos, sin):
    return y * cos + pltpu.roll(y, HEAD_DIM // 2, 1) * sin


def _rope64(y, cos, s1, s2):
    half = MLA_ROPE // 2
    return y * cos + pltpu.roll(y, half, 1) * s1 + pltpu.roll(y, LANES - half, 1) * s2


def _cast_weight_tile(w_ref, wb_scr):
    @pl.when(pl.program_id(1) == 0)
    def _():
        wb_scr[...] = w_ref[...].astype(BF16)


def _qkv_kernel(a_ref, w_ref, cos_ref, sin_ref, gq_ref, gk_ref, o_ref, wb_scr, *,
                nq_tiles, nqk_tiles, hpt, use_norm, use_rope, scale):
    j = pl.program_id(0)
    _cast_weight_tile(w_ref, wb_scr)
    acc = _dot(a_ref[...], wb_scr[...])

    @pl.when(j < nqk_tiles)
    def _():
        is_q = j < nq_tiles
        fac = jnp.where(is_q, scale, 1.0).astype(F32)
        g = jnp.where(is_q, gq_ref[...], gk_ref[...])
        for r in range(hpt):
            y = acc[:, r * HEAD_DIM:(r + 1) * HEAD_DIM]
            if use_norm:
                y = _rms(y, g)
            if use_rope:
                y = _rope128(y, cos_ref[...], sin_ref[...])
            o_ref[r] = (y * fac).astype(BF16)

    @pl.when(j >= nqk_tiles)
    def _():
        for r in range(hpt):
            o_ref[r] = acc[:, r * HEAD_DIM:(r + 1) * HEAD_DIM].astype(BF16)


def _qkv_proj(h, w, tabs, gq, gk, *, use_norm, use_rope, scale, tm=TM_BIG, tn=TN):
    t, d = h.shape
    n = w.shape[1]
    hpt = tn // HEAD_DIM
    cos, sin, tab_idx = tabs
    kern = functools.partial(
        _qkv_kernel, nq_tiles=N_HEADS * HEAD_DIM // tn, nqk_tiles=(N_HEADS + N_KV_HEADS) * HEAD_DIM // tn,
        hpt=hpt, use_norm=use_norm, use_rope=use_rope, scale=scale)
    return pl.pallas_call(
        kern,
        out_shape=jax.ShapeDtypeStruct((n // HEAD_DIM, t, HEAD_DIM), BF16),
        grid=(n // tn, t // tm),
        in_specs=[pl.BlockSpec((tm, d), lambda j, i: (i, 0)),
                  pl.BlockSpec((d, tn), lambda j, i: (0, j)),
                  pl.BlockSpec((tm, HEAD_DIM), lambda j, i: (tab_idx(i, tm), 0)),
                  pl.BlockSpec((tm, HEAD_DIM), lambda j, i: (tab_idx(i, tm), 0)),
                  pl.BlockSpec((1, HEAD_DIM), lambda j, i: (0, 0)),
                  pl.BlockSpec((1, HEAD_DIM), lambda j, i: (0, 0))],
        out_specs=pl.BlockSpec((hpt, tm, HEAD_DIM), lambda j, i: (j, i, 0)),
        scratch_shapes=[pltpu.VMEM((d, tn), BF16)],
        compiler_params=_cparams(2, 48),
        name="qkv",
    )(h, w, cos, sin, gq.reshape(1, HEAD_DIM), gk.reshape(1, HEAD_DIM))


def _res_kernel(a_ref, w_ref, x_ref, mod_ref, o_ref, wb_scr, *, gate_idx):
    _cast_weight_tile(w_ref, wb_scr)
    acc = _dot(a_ref[...], wb_scr[...])
    o_ref[...] = x_ref[...] + mod_ref[0, gate_idx:gate_idx + 1, :] * acc


def _proj_res(a, w, xres, mod, seg, *, rows, gate_idx, tm=TM_BIG, tn=TN):
    k, n = w.shape
    return pl.pallas_call(
        functools.partial(_res_kernel, gate_idx=gate_idx),
        out_shape=jax.ShapeDtypeStruct((rows, n), F32),
        grid=(n // tn, rows // tm),
        in_specs=[pl.BlockSpec((tm, k), lambda j, i: (i, 0)),
                  pl.BlockSpec((k, tn), lambda j, i: (0, j)),
                  pl.BlockSpec((tm, tn), lambda j, i: (i, j)),
                  pl.BlockSpec((1, 6, tn), lambda j, i: (seg(i, tm), 0, j))],
        out_specs=pl.BlockSpec((tm, tn), lambda j, i: (i, j)),
        scratch_shapes=[pltpu.VMEM((k, tn), BF16)],
        compiler_params=_cparams(2, 48),
        name="proj_res",
    )(a, w, xres, mod)


def _plain_kernel(a_ref, w_ref, o_ref, wb_scr):
    _cast_weight_tile(w_ref, wb_scr)
    o_ref[...] = _dot(a_ref[...], wb_scr[...]).astype(o_ref.dtype)


def _proj_plain(a, w, *, rows, tm=TM_BIG, tn=TN):
    k, n = w.shape
    return pl.pallas_call(
        _plain_kernel,
        out_shape=jax.ShapeDtypeStruct((rows, n), BF16),
        grid=(n // tn, rows // tm),
        in_specs=[pl.BlockSpec((tm, k), lambda j, i: (i, 0)),
                  pl.BlockSpec((k, tn), lambda j, i: (0, j))],
        out_specs=pl.BlockSpec((tm, tn), lambda j, i: (i, j)),
        scratch_shapes=[pltpu.VMEM((k, tn), BF16)],
        compiler_params=_cparams(2, 48),
        name="proj_plain",
    )(a, w)


def _ffn_down_kernel(a_ref, v_ref, ap_ref, an_ref, cw_ref, cb_ref, w_ref, x_ref, mod_ref, o_ref, g0_scr, g1_scr, *,
                     tm, n_tiles, sub_chunks, n_lat, n_ctx, lat_rows, gate_idx):
    r = pl.program_id(0)
    j = pl.program_id(1)

    @pl.when((r == 0) & (j == 0))
    def _():
        g1_scr[...] = jnp.zeros(g1_scr.shape, BF16)

    tile = jnp.minimum(r, n_tiles - 1)
    row = lax.broadcasted_iota(jnp.int32, (tm, LANES), 0)
    grow = tile * tm + row
    is_ctx = grow >= lat_rows
    first = ((grow & (n_ctx - 1)) == 0) & (is_ctx | ((grow & (n_lat - 1)) == 0))
    nxt = grow + 1
    last = ((nxt & (n_ctx - 1)) == 0) & (is_ctx | ((nxt & (n_lat - 1)) == 0))
    top = row == 0
    bot = row == tm - 1

    def step(g_write, g_read):
        for c in range(sub_chunks):
            sl = slice(c * LANES, (c + 1) * LANES)
            a = a_ref[:, sl].astype(F32)
            a_dn = jnp.where(top, ap_ref[7:8, sl].astype(F32), pltpu.roll(a, 1, 0))
            a_dn = jnp.where(first, 0.0, a_dn)
            a_up = jnp.where(bot, an_ref[0:1, sl].astype(F32), pltpu.roll(a, tm - 1, 0))
            a_up = jnp.where(last, 0.0, a_up)
            cv = cw_ref[0:1, sl] * a_dn + cw_ref[1:2, sl] * a + cw_ref[2:3, sl] * a_up + cb_ref[:, sl]
            dst = pl.ds(pl.multiple_of((j * sub_chunks + c) * LANES, LANES), LANES)
            g_write[:, dst] = (cv / (1.0 + jnp.exp(-cv)) * v_ref[:, sl].astype(F32)).astype(BF16)
        acc = _dot(g_read[...], w_ref[...])
        o_ref[...] = x_ref[...] + mod_ref[0, gate_idx:gate_idx + 1, :] * acc

    pl.when(r % 2 == 0)(lambda: step(g0_scr, g1_scr))
    pl.when(r % 2 == 1)(lambda: step(g1_scr, g0_scr))


def _ffn_down(u, conv_w, conv_b, w, xres, mod, seg, *, rows, n_lat, n_ctx, lat_rows, gate_idx, tm=TM, tn=TN):
    dff, n = w.shape
    nb8 = u.shape[0] // 8
    n_tiles = rows // tm
    n_j = n // tn
    ts = dff // n_j
    sub_chunks = ts // LANES
    assert sub_chunks * n_j * LANES == dff
    kern = functools.partial(_ffn_down_kernel, tm=tm, n_tiles=n_tiles, sub_chunks=sub_chunks, n_lat=n_lat,
                             n_ctx=n_ctx, lat_rows=lat_rows, gate_idx=gate_idx)

    def cur(r):
        return jnp.minimum(r, n_tiles - 1)

    def prev(r):
        return jnp.maximum(r - 1, 0)

    return pl.pallas_call(
        kern,
        out_shape=jax.ShapeDtypeStruct((rows, n), F32),
        grid=(n_tiles + 1, n_j),
        in_specs=[pl.BlockSpec((tm, ts), lambda r, j: (cur(r), j)),
                  pl.BlockSpec((tm, ts), lambda r, j: (cur(r), n_j + j)),
                  pl.BlockSpec((8, ts), lambda r, j: (jnp.maximum(cur(r) * (tm // 8) - 1, 0), j)),
                  pl.BlockSpec((8, ts), lambda r, j: (jnp.minimum((cur(r) + 1) * (tm // 8), nb8 - 1), j)),
                  pl.BlockSpec((3, ts), lambda r, j: (0, j)),
                  pl.BlockSpec((1, ts), lambda r, j: (0, j)),
                  pl.BlockSpec((dff, tn), lambda r, j: (0, j)),
                  pl.BlockSpec((tm, tn), lambda r, j: (prev(r), j)),
                  pl.BlockSpec((1, 6, tn), lambda r, j: (seg(prev(r), tm), 0, j))],
        out_specs=pl.BlockSpec((tm, tn), lambda r, j: (prev(r), jnp.where(r == 0, 0, j))),
        scratch_shapes=[pltpu.VMEM((tm, dff), BF16), pltpu.VMEM((tm, dff), BF16)],
        compiler_params=_cparams(2, 56),
        name="ffn_down",
    )(u, u, u, u, conv_w, conv_b.reshape(1, dff), w, xres, mod)


def _softmax_first(s, v, sink_tile):
    n = s.shape[1]
    if sink_tile is not None:
        s = jnp.concatenate([s, sink_tile], axis=1)
    m = jnp.max(s, axis=1, keepdims=True)
    p = jnp.exp(s - m)
    return m, jnp.sum(p, axis=1, keepdims=True), _dot(p[:, :n].astype(BF16), v)


def _softmax_empty(rows, dv):
    return jnp.full((rows, 1), NEG_INF, F32), jnp.zeros((rows, 1), F32), jnp.zeros((rows, dv), F32)


def _softmax_update(carry, s, v, sink_tile=None):
    m, l, acc = carry
    n = s.shape[1]
    if sink_tile is not None:
        s = jnp.concatenate([s, sink_tile], axis=1)
    m_new = jnp.maximum(m, jnp.max(s, axis=1, keepdims=True))
    p = jnp.exp(s - m_new)
    alpha = jnp.exp(m - m_new)
    return m_new, alpha * l + jnp.sum(p, axis=1, keepdims=True), alpha * acc + _dot(p[:, :n].astype(BF16), v)


def _attn_kernel(*refs, mode, r_heads, n_lat, tk, has_sink, with_ctx, rows, nt):
    refs = list(refs)
    q_ref, kc_ref, vc_ref, kl_ref, vl_ref = refs[:5]
    pos = 5
    sink_ref = bias_ref = None
    if has_sink:
        sink_ref = refs[pos]
        pos += 1
    if mode in ("window", "na"):
        bias_ref = refs[pos]
        pos += 1
    o_ref = refs[-1]
    t = pl.program_id(2)
    tq = q_ref.shape[1]
    dq = q_ref.shape[2]
    m_rows = r_heads * tq

    def prologue():
        q = q_ref[...].reshape(m_rows, dq)
        sink_tile = None
        if has_sink:
            sink_tile = jnp.concatenate([jnp.broadcast_to(sink_ref[r], (tq, LANES)) for r in range(r_heads)], axis=0)
        return q, sink_tile, _dot_nt(q, kc_ref[0])

    def finish(carry):
        _, l, acc = carry
        o = acc / l
        dv = acc.shape[-1]
        for r in range(r_heads):
            o_ref[:, r * dv:(r + 1) * dv] = o[r * tq:(r + 1) * tq].astype(o_ref.dtype)

    def latent_tile():
        q, sink_tile, s_ctx = prologue()
        if mode == "dense":
            n_chunks = n_lat // tk
            unroll = math.gcd(KV_UNROLL, n_chunks)

            def chunk(c):
                return pl.ds(pl.multiple_of(c * tk, tk), tk)

            carry = _softmax_first(s_ctx, vc_ref[0], sink_tile)

            def body(c, carry):
                scores = [_dot_nt(q, kl_ref[0, chunk(unroll * c + u), :]) for u in range(unroll)]
                for u in range(unroll):
                    carry = _softmax_update(carry, scores[u], vl_ref[0, chunk(unroll * c + u), :])
                return carry

            carry = lax.fori_loop(0, n_chunks // unroll, body, carry)
        else:
            variant = jnp.where(t == 0, 0, jnp.where(t == nt - 1, 2, 1))
            if mode == "window":
                span = tq + 2 * WINDOW
                start = pl.multiple_of(jnp.clip(t * tq - WINDOW, 0, n_lat - span), WINDOW)
                bias = jnp.concatenate([bias_ref[variant]] * r_heads, axis=0)
            else:
                span = NA_KEY_ROWS * GRID_W
                ks_row = jnp.clip(t * (tq // GRID_W) - NA_KH // 2, 0, rows - NA_KEY_ROWS)
                start = pl.multiple_of(ks_row * GRID_W, GRID_W)
                bias = bias_ref[variant].reshape(m_rows, span)
            sl = pl.ds(start, span)
            s_loc = _dot_nt(q, kl_ref[0, sl, :]) + bias
            s_all = jnp.concatenate([s_ctx, s_loc], axis=1)
            v_all = jnp.concatenate([vc_ref[0], vl_ref[0, sl, :]], axis=0)
            carry = _softmax_update(_softmax_empty(m_rows, v_all.shape[1]), s_all, v_all, sink_tile)
        finish(carry)

    def context_tile():
        _, sink_tile, s_ctx = prologue()
        finish(_softmax_first(s_ctx, vc_ref[0], sink_tile))

    if with_ctx:
        pl.when(t < nt)(latent_tile)
        pl.when(t == nt)(context_tile)
    else:
        latent_tile()


def _window_bias():
    span = TQ + 2 * WINDOW
    d = np.arange(TQ)[:, None] - np.arange(span)[None, :]
    offsets = (0, WINDOW, span - TQ)
    return np.stack([np.where(np.abs(d + o) <= WINDOW, 0.0, NEG_INF) for o in offsets]).astype(np.float32)


def _attention(qa, ka, va, *, mode, with_ctx, batch, n_lat, n_ctx, groups, r_heads, q_base, k_base, v_base,
               sink=None, bias=None, tq=TQ, tk=TK):
    dq = qa.shape[2]
    dv = va.shape[2]
    lat_rows = batch * n_lat
    nq = n_lat // tq
    ctx_blk = lat_rows // n_ctx
    assert not with_ctx or tq == n_ctx

    def qrow(b, g, t):
        return jnp.where(t < nq, b * nq + t, lat_rows // tq + b) if with_ctx else b * nq + t

    in_specs = [pl.BlockSpec((r_heads, tq, dq), lambda b, g, t: (q_base + g, qrow(b, g, t), 0)),
                pl.BlockSpec((1, n_ctx, dq), lambda b, g, t: (k_base + g, ctx_blk + b, 0)),
                pl.BlockSpec((1, n_ctx, dv), lambda b, g, t: (v_base + g, ctx_blk + b, 0)),
                pl.BlockSpec((1, n_lat, dq), lambda b, g, t: (k_base + g, b, 0)),
                pl.BlockSpec((1, n_lat, dv), lambda b, g, t: (v_base + g, b, 0))]
    args = [qa, ka, va, ka, va]
    if sink is not None:
        in_specs.append(pl.BlockSpec((r_heads, 1, LANES), lambda b, g, t: (g, 0, 0)))
        args.append(jnp.full((sink.shape[0], 1, LANES), NEG_INF, F32).at[:, 0, 0].set(sink))
    if mode == "window":
        in_specs.append(pl.BlockSpec(bias.shape, lambda b, g, t: (0, 0, 0)))
        args.append(bias)
    if mode == "na":
        in_specs.append(pl.BlockSpec((3, r_heads, tq, NA_KEY_ROWS * GRID_W), lambda b, g, t: (0, g, 0, 0)))
        args.append(bias)
    kern = functools.partial(_attn_kernel, mode=mode, r_heads=r_heads, n_lat=n_lat, tk=tk, has_sink=sink is not None,
                             with_ctx=with_ctx, rows=n_lat // GRID_W, nt=nq)
    out_rows = lat_rows + (batch * n_ctx if with_ctx else 0)
    return pl.pallas_call(
        kern,
        out_shape=jax.ShapeDtypeStruct((out_rows, groups * r_heads * dv), BF16),
        grid=(batch, groups, nq + (1 if with_ctx else 0)),
        in_specs=in_specs,
        out_specs=pl.BlockSpec((tq, r_heads * dv), lambda b, g, t: (qrow(b, g, t), g)),
        compiler_params=_cparams(3, 48),
        name="attn_" + mode,
    )(*args)


def _na_variants(rows, q_rows):
    nt = rows // q_rows

    def structure(t):
        ks_row = int(np.clip(t * q_rows - NA_KH // 2, 0, rows - NA_KEY_ROWS))
        out = []
        for qr in range(q_rows):
            r = t * q_rows + qr
            rs = int(np.clip(r - NA_KH // 2, 0, rows - NA_KH))
            out.append([(rs <= ks_row + j < rs + NA_KH, ks_row + j - r + NA_KH - 1) for j in range(NA_KEY_ROWS)])
        return out

    reps = [structure(0), structure(1), structure(nt - 1)]
    for t in range(nt):
        want = reps[0] if t == 0 else reps[2] if t == nt - 1 else reps[1]
        got = structure(t)
        for wq, gq in zip(want, got):
            for (wv, wa), (gv, ga) in zip(wq, gq):
                assert wv == gv and (not wv or wa == ga), "neighbourhood tile variants do not cover this grid"
    return reps


def _na_bias_kernel(rb_ref, o_ref, *, variants):
    c = lax.broadcasted_iota(jnp.int32, (GRID_W, LANES), 0)
    lane = lax.broadcasted_iota(jnp.int32, (GRID_W, LANES), 1)
    kc = lane & (GRID_W - 1)
    cs = jnp.clip(c - NA_KW // 2, 0, GRID_W - NA_KW)
    col_ok = (kc >= cs) & (kc < cs + NA_KW)
    low = lane < GRID_W
    neg = jnp.full((GRID_W, LANES), NEG_INF, F32)
    for v, var in enumerate(variants):
        for qr, krows in enumerate(var):
            for jp in range(NA_KEY_ROWS // 2):
                halves = []
                for hf in range(2):
                    valid, a = krows[2 * jp + hf]
                    if valid:
                        x = jnp.broadcast_to(rb_ref[0, a:a + 1, :], (GRID_W, LANES))
                        shift = (LANES - (NA_KW - 1) + GRID_W * hf) % LANES
                        halves.append(pltpu.roll(x, shift, 1, stride=1, stride_axis=0))
                    else:
                        halves.append(neg)
                tile = jnp.where(col_ok, jnp.where(low, halves[0], halves[1]), NEG_INF)
                o_ref[v, 0, qr * GRID_W:(qr + 1) * GRID_W, jp * LANES:(jp + 1) * LANES] = tile


def _na_bias(rel_bias, rows):
    q_rows = TQ // GRID_W
    variants = _na_variants(rows, q_rows)
    nh, na, nb = rel_bias.shape
    rb = jnp.zeros((nh, 16, LANES), F32).at[:, :na, :nb].set(rel_bias)
    span = NA_KEY_ROWS * GRID_W
    return pl.pallas_call(
        functools.partial(_na_bias_kernel, variants=variants),
        out_shape=jax.ShapeDtypeStruct((3, nh, TQ, span), F32),
        grid=(nh,),
        in_specs=[pl.BlockSpec((1, 16, LANES), lambda h: (h, 0, 0))],
        out_specs=pl.BlockSpec((3, 1, TQ, span), lambda h: (0, h, 0, 0)),
        compiler_params=_cparams(1, 32),
        name="na_bias",
    )(rb)


def _mla_dq_kernel(a_ref, w_ref, g_ref, o_ref):
    o_ref[...] = _rms(_dot(a_ref[...], w_ref[...]), g_ref[...]).astype(BF16)


def _mla_dq(h, w, g, tm=TM):
    t, d = h.shape
    n = w.shape[1]
    return pl.pallas_call(
        _mla_dq_kernel,
        out_shape=jax.ShapeDtypeStruct((t, n), BF16),
        grid=(t // tm,),
        in_specs=[pl.BlockSpec((tm, d), lambda i: (i, 0)),
                  pl.BlockSpec((d, n), lambda i: (0, 0)),
                  pl.BlockSpec((1, n), lambda i: (0, 0))],
        out_specs=pl.BlockSpec((tm, n), lambda i: (i, 0)),
        compiler_params=_cparams(1, 48),
        name="mla_dq",
    )(h, w, g.reshape(1, n))


def _mla_uq_kernel(a_ref, w_ref, cos_ref, s1_ref, s2_ref, o_ref, *, hpt, scale):
    acc = _dot(a_ref[...], w_ref[...])
    for r in range(hpt):
        base = r * MLA_QK
        o_ref[r, :, :MLA_NOPE] = (acc[:, base:base + MLA_NOPE] * scale).astype(BF16)
        pe = _rope64(acc[:, base + MLA_NOPE:base + MLA_QK], cos_ref[...], s1_ref[...], s2_ref[...])
        o_ref[r, :, MLA_NOPE:] = (pe * scale).astype(BF16)


def _mla_uq(cq, w, tabs, *, scale, tm=TM_BIG, tn=TN):
    t, k = cq.shape
    n = w.shape[1]
    hpt = tn // MLA_QK
    cos, s1, s2, tab_idx = tabs
    tab_spec = pl.BlockSpec((tm, LANES), lambda i, j: (tab_idx(i, tm), 0))
    return pl.pallas_call(
        functools.partial(_mla_uq_kernel, hpt=hpt, scale=scale),
        out_shape=jax.ShapeDtypeStruct((n // MLA_QK, t, MLA_QK), BF16),
        grid=(t // tm, n // tn),
        in_specs=[pl.BlockSpec((tm, k), lambda i, j: (i, 0)),
                  pl.BlockSpec((k, tn), lambda i, j: (0, j)),
                  tab_spec, tab_spec, tab_spec],
        out_specs=pl.BlockSpec((hpt, tm, MLA_QK), lambda i, j: (j, i, 0)),
        compiler_params=_cparams(2, 40),
        name="mla_uq",
    )(cq, w, cos, s1, s2)


def _mla_dkv_kernel(a_ref, w_ref, g_ref, cos_ref, s1_ref, s2_ref, ckv_ref, kpe_ref, *, rank):
    acc = _dot(a_ref[...], w_ref[...])
    ckv_ref[...] = _rms(acc[:, :rank], g_ref[...]).astype(BF16)
    kpe_ref[...] = _rope64(acc[:, rank:], cos_ref[...], s1_ref[...], s2_ref[...]).astype(BF16)


def _mla_dkv(h, w, g, tabs, *, rank, tm=TM):
    t, d = h.shape
    n = w.shape[1]
    cos, s1, s2, tab_idx = tabs
    tab_spec = pl.BlockSpec((tm, LANES), lambda i: (tab_idx(i, tm), 0))
    return pl.pallas_call(
        functools.partial(_mla_dkv_kernel, rank=rank),
        out_shape=(jax.ShapeDtypeStruct((t, rank), BF16), jax.ShapeDtypeStruct((t, LANES), BF16)),
        grid=(t // tm,),
        in_specs=[pl.BlockSpec((tm, d), lambda i: (i, 0)),
                  pl.BlockSpec((d, n), lambda i: (0, 0)),
                  pl.BlockSpec((1, rank), lambda i: (0, 0)),
                  tab_spec, tab_spec, tab_spec],
        out_specs=(pl.BlockSpec((tm, rank), lambda i: (i, 0)),
                   pl.BlockSpec((tm, LANES), lambda i: (i, 0))),
        compiler_params=_cparams(1, 40),
        name="mla_dkv",
    )(h, w, g.reshape(1, rank), cos, s1, s2)


def _mla_ukv_kernel(a_ref, w_ref, kpe_ref, k_ref, v_ref, *, hpt):
    acc = _dot(a_ref[...], w_ref[...])
    for r in range(hpt):
        base = r * (MLA_NOPE + MLA_V)
        k_ref[r, :, :MLA_NOPE] = acc[:, base:base + MLA_NOPE].astype(BF16)
        k_ref[r, :, MLA_NOPE:] = kpe_ref[...]
        v_ref[r] = acc[:, base + MLA_NOPE:base + MLA_NOPE + MLA_V].astype(BF16)


def _mla_ukv(ckv, w, kpe, *, tm=TM_BIG, tn=TN):
    t, k = ckv.shape
    n = w.shape[1]
    hpt = tn // (MLA_NOPE + MLA_V)
    nh = n // (MLA_NOPE + MLA_V)
    return pl.pallas_call(
        functools.partial(_mla_ukv_kernel, hpt=hpt),
        out_shape=(jax.ShapeDtypeStruct((nh, t, MLA_QK), BF16), jax.ShapeDtypeStruct((nh, t, MLA_V), BF16)),
        grid=(t // tm, n // tn),
        in_specs=[pl.BlockSpec((tm, k), lambda i, j: (i, 0)),
                  pl.BlockSpec((k, tn), lambda i, j: (0, j)),
                  pl.BlockSpec((tm, LANES), lambda i, j: (i, 0))],
        out_specs=(pl.BlockSpec((hpt, tm, MLA_QK), lambda i, j: (j, i, 0)),
                   pl.BlockSpec((hpt, tm, MLA_V), lambda i, j: (j, i, 0))),
        compiler_params=_cparams(2, 40),
        name="mla_ukv",
    )(ckv, w, kpe)


def _axial_angles(n_tok, rot_dim):
    n_freq = rot_dim // 4
    freqs = ROPE_THETA ** (-jnp.arange(n_freq, dtype=F32) / n_freq)
    t = jnp.arange(n_tok)
    row = (t // GRID_W).astype(F32)
    col = (t % GRID_W).astype(F32)
    return jnp.concatenate([row[:, None] * freqs, col[:, None] * freqs], axis=-1)


def _rope_tables(n_lat, tail):
    ang = _axial_angles(n_lat, HEAD_DIM)
    cos, sin = jnp.cos(ang), jnp.sin(ang)
    cos_t = jnp.concatenate([cos, cos], axis=-1)
    sin_t = jnp.concatenate([-sin, sin], axis=-1)
    cos_t = jnp.concatenate([cos_t, jnp.ones((tail, HEAD_DIM), F32)], axis=0)
    sin_t = jnp.concatenate([sin_t, jnp.zeros((tail, HEAD_DIM), F32)], axis=0)
    return cos_t, sin_t


def _rope64_tables(n_lat, tail):
    ang = _axial_angles(n_lat, MLA_ROPE)
    cos, sin = jnp.cos(ang), jnp.sin(ang)
    half = MLA_ROPE // 2
    z = jnp.zeros((n_lat, half), F32)
    zpad = jnp.zeros((n_lat, LANES - MLA_ROPE), F32)
    cos_t = jnp.concatenate([cos, cos, zpad], axis=-1)
    s1 = jnp.concatenate([z, sin, zpad], axis=-1)
    s2 = jnp.concatenate([-sin, z, zpad], axis=-1)
    ident = jnp.concatenate([jnp.ones((tail, MLA_ROPE), F32), jnp.zeros((tail, LANES - MLA_ROPE), F32)], axis=-1)
    zt = jnp.zeros((tail, LANES), F32)
    return (jnp.concatenate([cos_t, ident], axis=0), jnp.concatenate([s1, zt], axis=0),
            jnp.concatenate([s2, zt], axis=0))


def kernel(x, c, ctx, c_ctx, l0_ada_w, l0_ada_b, l0_attn_norm_g, l0_w_qkv, l0_q_norm_g, l0_k_norm_g, l0_w_o, l0_ffn_norm_g, l0_ffn_w_up, l0_ffn_conv_w, l0_ffn_conv_b, l0_ffn_w_down, l1_ada_w, l1_ada_b, l1_attn_norm_g, l1_w_qkv, l1_sink, l1_w_o, l1_ffn_norm_g, l1_ffn_w_up, l1_ffn_conv_w, l1_ffn_conv_b, l1_ffn_w_down, l2_ada_w, l2_ada_b, l2_attn_norm_g, l2_w_qkv, l2_rel_bias, l2_w_o, l2_ffn_norm_g, l2_ffn_w_up, l2_ffn_conv_w, l2_ffn_conv_b, l2_ffn_w_down, l3_ada_w, l3_ada_b, l3_attn_norm_g, l3_w_dq, l3_q_norm_g, l3_w_uq, l3_w_dkv, l3_kv_norm_g, l3_w_ukv, l3_w_o, l3_ffn_norm_g, l3_ffn_w_up, l3_ffn_conv_w, l3_ffn_conv_b, l3_ffn_w_down, final_norm_g):
    batch, n_lat, d = x.shape
    n_ctx = ctx.shape[1]
    lat_rows = batch * n_lat
    t_rows = lat_rows + batch * n_ctx
    rows = n_lat // GRID_W
    assert batch + 1 <= 8 and n_ctx == TQ and n_lat % TM_BIG == 0 and (batch * n_ctx) % TM_BIG == 0
    assert n_lat & (n_lat - 1) == 0 and rows >= NA_KEY_ROWS and n_lat >= TQ + 2 * WINDOW

    def seg(i, tile):
        return jnp.minimum(i * tile // n_lat, batch)

    def tab_idx(i, tile):
        return jnp.where(i < lat_rows // tile, i % (n_lat // tile), n_lat // tile)

    cos_t, sin_t = _rope_tables(n_lat, TM_BIG)
    tabs128 = (cos_t, sin_t, tab_idx)
    tabs64 = _rope64_tables(n_lat, TM_BIG) + (tab_idx,)
    ones_g = jnp.ones((HEAD_DIM,), F32)

    xa = jnp.concatenate([x.reshape(lat_rows, d), ctx.reshape(batch * n_ctx, d)], axis=0)
    c8 = jnp.zeros((8, d), F32).at[:batch].set(c).at[batch].set(c_ctx)

    ada = [(l0_ada_w, l0_ada_b), (l1_ada_w, l1_ada_b), (l2_ada_w, l2_ada_b), (l3_ada_w, l3_ada_b)]
    attn_g = [l0_attn_norm_g, l1_attn_norm_g, l2_attn_norm_g, l3_attn_norm_g]
    ffn_g = [l0_ffn_norm_g, l1_ffn_norm_g, l2_ffn_norm_g, l3_ffn_norm_g]
    w_o = [l0_w_o, l1_w_o, l2_w_o, l3_w_o]
    ffn_p = [(l0_ffn_w_up, l0_ffn_conv_w, l0_ffn_conv_b, l0_ffn_w_down),
             (l1_ffn_w_up, l1_ffn_conv_w, l1_ffn_conv_b, l1_ffn_w_down),
             (l2_ffn_w_up, l2_ffn_conv_w, l2_ffn_conv_b, l2_ffn_w_down),
             (l3_ffn_w_up, l3_ffn_conv_w, l3_ffn_conv_b, l3_ffn_w_down)]
    w_qkv = [l0_w_qkv, l1_w_qkv, l2_w_qkv]
    gqa_scale = HEAD_DIM ** -0.5
    common = dict(batch=batch, n_lat=n_lat, n_ctx=n_ctx)
    gqa = dict(groups=N_KV_HEADS, r_heads=Q_PER_KV, q_base=0, k_base=N_HEADS, v_base=N_HEADS + N_KV_HEADS,
               **common)
    depth = 4

    for layer in range(depth):
        need_ctx = layer < depth - 1
        mod = _ada(c8, *ada[layer]).reshape(8, 6, d)
        h = _normmod(xa, attn_g[layer], mod, seg, rows=t_rows, shift_idx=0, scale_idx=1)

        if layer < 3:
            w = w_qkv[layer]
            if layer == 0:
                qkv = _qkv_proj(h, w, tabs128, l0_q_norm_g, l0_k_norm_g, use_norm=True, use_rope=True,
                                scale=gqa_scale)
            elif layer == 1:
                qkv = _qkv_proj(h, w, tabs128, ones_g, ones_g, use_norm=False, use_rope=True, scale=gqa_scale)
            else:
                qkv = _qkv_proj(h, w, tabs128, ones_g, ones_g, use_norm=False, use_rope=False, scale=gqa_scale)
            if layer == 0:
                o = _attention(qkv, qkv, qkv, mode="dense", with_ctx=True, **gqa)
            elif layer == 1:
                o = _attention(qkv, qkv, qkv, mode="window", with_ctx=True, sink=l1_sink,
                               bias=jnp.asarray(_window_bias()), **gqa)
            else:
                o = _attention(qkv, qkv, qkv, mode="na", with_ctx=True, bias=_na_bias(l2_rel_bias, rows), **gqa)
        else:
            mla_scale = (MLA_NOPE + MLA_ROPE) ** -0.5
            rq = l3_w_uq.shape[0]
            w_uq = l3_w_uq.reshape(rq, N_HEADS, MLA_NOPE + MLA_ROPE)
            w_uq = jnp.pad(w_uq, ((0, 0), (0, 0), (0, MLA_QK - MLA_NOPE - MLA_ROPE))).reshape(rq, N_HEADS * MLA_QK)
            rank = l3_kv_norm_g.shape[0]
            w_dkv = jnp.pad(l3_w_dkv, ((0, 0), (0, LANES - MLA_ROPE)))
            cq = _mla_dq(h, l3_w_dq.astype(BF16), l3_q_norm_g)
            qh = _mla_uq(cq, w_uq.astype(BF16), tabs64, scale=mla_scale)
            ckv, kpe = _mla_dkv(h, w_dkv.astype(BF16), l3_kv_norm_g, tabs64, rank=rank)
            kh, vh = _mla_ukv(ckv, l3_w_ukv.astype(BF16), kpe)
            o = _attention(qh, kh, vh, mode="dense", with_ctx=False, groups=N_HEADS, r_heads=1, q_base=0, k_base=0,
                           v_base=0, tq=4 * TQ, **common)

        out_rows = t_rows if need_ctx else lat_rows
        xa = _proj_res(o, w_o[layer], xa, mod, seg, rows=out_rows, gate_idx=2)
        h2 = _normmod(xa, ffn_g[layer], mod, seg, rows=out_rows, shift_idx=3, scale_idx=4)
        w_up, conv_w, conv_b, w_down = ffn_p[layer]
        u = _proj_plain(h2, w_up, rows=out_rows)
        xa = _ffn_down(u, conv_w, conv_b, w_down.astype(BF16), xa, mod, seg, rows=out_rows, n_lat=n_lat,
                       n_ctx=n_ctx, lat_rows=lat_rows, gate_idx=5)

    return _final_norm(xa, final_norm_g, rows=lat_rows).reshape(batch, n_lat, d)
```

```python
import functools
import math

import jax
import jax.numpy as jnp
import numpy as np
from jax import lax
from jax.experimental import pallas as pl
from jax.experimental.pallas import tpu as pltpu

F32 = jnp.float32
BF16 = jnp.bfloat16

GRID_W = 64
HEAD_DIM = 128
N_HEADS = 32
N_KV_HEADS = 8
Q_PER_KV = 4
WINDOW = 128
NA_KH = 8
NA_KW = 16
NA_KEY_ROWS = 12
MLA_NOPE = 128
MLA_ROPE = 64
MLA_V = 128
MLA_QK = 256
ROPE_THETA = 10000.0
EPS = 1e-6
NEG_INF = -1e30
LANES = 128
TQ = 256
TK = 1024
KV_UNROLL = 8
TM_BIG = 1024
TM = 512
TM_NORM = 256
TN = 512
MIB = 1024 * 1024


def _cparams(n_axes, vmem_mib):
    return pltpu.CompilerParams(dimension_semantics=("arbitrary",) * n_axes,
                                vmem_limit_bytes=vmem_mib * MIB)


def _dot(a, b):
    return jnp.dot(a, b, preferred_element_type=F32)


def _dot_nt(a, b):
    return lax.dot_general(a, b, (((1,), (1,)), ((), ())), preferred_element_type=F32)


def _rms(y, g):
    return y * lax.rsqrt(jnp.mean(y * y, axis=-1, keepdims=True) + EPS) * g


def _ada_kernel(c_ref, w_ref, b_ref, o_ref):
    x = c_ref[...]
    s = x / (1.0 + jnp.exp(-x))
    o_ref[...] = _dot(s.astype(BF16), w_ref[...].astype(BF16)) + b_ref[...]


def _ada(c8, w, b, tn=TN):
    d, n = w.shape
    return pl.pallas_call(
        _ada_kernel,
        out_shape=jax.ShapeDtypeStruct((8, n), F32),
        grid=(n // tn,),
        in_specs=[pl.BlockSpec((8, d), lambda j: (0, 0)),
                  pl.BlockSpec((d, tn), lambda j: (0, j)),
                  pl.BlockSpec((1, tn), lambda j: (0, j))],
        out_specs=pl.BlockSpec((8, tn), lambda j: (0, j)),
        compiler_params=_cparams(1, 40),
        name="ada",
    )(c8, w, b.reshape(1, n))


def _stream_specs(x, block, row_tile, col_tile):
    if not isinstance(x, tuple):
        return [pl.BlockSpec(block, lambda *ids: (row_tile(*ids), col_tile(*ids)))], [x], None
    lat_tiles = x[0].shape[0] // block[0]
    specs = [pl.BlockSpec(block, lambda *ids: (jnp.minimum(row_tile(*ids), lat_tiles - 1), col_tile(*ids))),
             pl.BlockSpec(block, lambda *ids: (jnp.maximum(row_tile(*ids) - lat_tiles, 0), col_tile(*ids)))]
    return specs, list(x), lat_tiles


def _stream_tile(refs, lat_tiles, row_tile, fn):
    if lat_tiles is None:
        fn(refs[0])
    else:
        pl.when(row_tile < lat_tiles)(lambda: fn(refs[0]))
        pl.when(row_tile >= lat_tiles)(lambda: fn(refs[1]))


def _normmod_kernel(*refs, lat_tiles, shift_idx, scale_idx):
    g_ref, mod_ref, o_ref = refs[-3:]

    def compute(x_ref):
        y = _rms(x_ref[...], g_ref[...])
        y = y * (1.0 + mod_ref[0, scale_idx:scale_idx + 1, :]) + mod_ref[0, shift_idx:shift_idx + 1, :]
        o_ref[...] = y.astype(o_ref.dtype)

    _stream_tile(refs[:-3], lat_tiles, pl.program_id(0), compute)


def _normmod(x, g, mod, seg, *, rows, shift_idx, scale_idx, tm=TM_NORM):
    d = g.shape[0]
    x_specs, x_args, lat_tiles = _stream_specs(x, (tm, d), lambda i: i, lambda i: 0)
    return pl.pallas_call(
        functools.partial(_normmod_kernel, lat_tiles=lat_tiles, shift_idx=shift_idx, scale_idx=scale_idx),
        out_shape=jax.ShapeDtypeStruct((rows, d), BF16),
        grid=(rows // tm,),
        in_specs=x_specs + [pl.BlockSpec((1, d), lambda i: (0, 0)),
                            pl.BlockSpec((1, 6, d), lambda i: (seg(i, tm), 0, 0))],
        out_specs=pl.BlockSpec((tm, d), lambda i: (i, 0)),
        compiler_params=_cparams(1, 40),
        name="normmod",
    )(*x_args, g.reshape(1, d), mod)


def _final_norm_kernel(x_ref, g_ref, o_ref):
    o_ref[...] = _rms(x_ref[...], g_ref[...])


def _final_norm(x, g, *, rows, tm=TM_NORM):
    d = x.shape[1]
    return pl.pallas_call(
        _final_norm_kernel,
        out_shape=jax.ShapeDtypeStruct((rows, d), F32),
        grid=(rows // tm,),
        in_specs=[pl.BlockSpec((tm, d), lambda i: (i, 0)),
                  pl.BlockSpec((1, d), lambda i: (0, 0))],
        out_specs=pl.BlockSpec((tm, d), lambda i: (i, 0)),
        compiler_params=_cparams(1, 32),
        name="final_norm",
    )(x, g.reshape(1, d))


def _rope128(y, cos, sin):
    return y * cos + pltpu.roll(y, HEAD_DIM // 2, 1) * sin


def _rope64(y, cos, s1, s2):
    half = MLA_ROPE // 2
    return y * cos + pltpu.roll(y, half, 1) * s1 + pltpu.roll(y, LANES - half, 1) * s2


def _cast_weight_tile(w_ref, wb_scr):
    @pl.when(pl.program_id(1) == 0)
    def _():
        wb_scr[...] = w_ref[...].astype(BF16)


def _qkv_kernel(a_ref, w_ref, cos_ref, sin_ref, gq_ref, gk_ref, o_ref, wb_scr, *,
                nq_tiles, nqk_tiles, hpt, use_norm, use_rope, scale):
    j = pl.program_id(0)
    _cast_weight_tile(w_ref, wb_scr)
    acc = _dot(a_ref[...], wb_scr[...])

    @pl.when(j < nqk_tiles)
    def _():
        is_q = j < nq_tiles
        fac = jnp.where(is_q, scale, 1.0).astype(F32)
        g = jnp.where(is_q, gq_ref[...], gk_ref[...])
        for r in range(hpt):
            y = acc[:, r * HEAD_DIM:(r + 1) * HEAD_DIM]
            if use_norm:
                y = _rms(y, g)
            if use_rope:
                y = _rope128(y, cos_ref[...], sin_ref[...])
            o_ref[r] = (y * fac).astype(BF16)

    @pl.when(j >= nqk_tiles)
    def _():
        for r in range(hpt):
            o_ref[r] = acc[:, r * HEAD_DIM:(r + 1) * HEAD_DIM].astype(BF16)


def _qkv_proj(h, w, tabs, gq, gk, *, use_norm, use_rope, scale, tm=TM_BIG, tn=TN):
    t, d = h.shape
    n = w.shape[1]
    hpt = tn // HEAD_DIM
    cos, sin, tab_idx = tabs
    kern = functools.partial(
        _qkv_kernel, nq_tiles=N_HEADS * HEAD_DIM // tn, nqk_tiles=(N_HEADS + N_KV_HEADS) * HEAD_DIM // tn,
        hpt=hpt, use_norm=use_norm, use_rope=use_rope, scale=scale)
    return pl.pallas_call(
        kern,
        out_shape=jax.ShapeDtypeStruct((n // HEAD_DIM, t, HEAD_DIM), BF16),
        grid=(n // tn, t // tm),
        in_specs=[pl.BlockSpec((tm, d), lambda j, i: (i, 0)),
                  pl.BlockSpec((d, tn), lambda j, i: (0, j)),
                  pl.BlockSpec((tm, HEAD_DIM), lambda j, i: (tab_idx(i, tm), 0)),
                  pl.BlockSpec((tm, HEAD_DIM), lambda j, i: (tab_idx(i, tm), 0)),
                  pl.BlockSpec((1, HEAD_DIM), lambda j, i: (0, 0)),
                  pl.BlockSpec((1, HEAD_DIM), lambda j, i: (0, 0))],
        out_specs=pl.BlockSpec((hpt, tm, HEAD_DIM), lambda j, i: (j, i, 0)),
        scratch_shapes=[pltpu.VMEM((d, tn), BF16)],
        compiler_params=_cparams(2, 48),
        name="qkv",
    )(h, w, cos, sin, gq.reshape(1, HEAD_DIM), gk.reshape(1, HEAD_DIM))


def _res_kernel(a_ref, w_ref, *refs, lat_tiles, gate_idx):
    mod_ref, o_ref, wb_scr = refs[-3:]
    _cast_weight_tile(w_ref, wb_scr)
    acc = _dot(a_ref[...], wb_scr[...])

    def finish(x_ref):
        o_ref[...] = x_ref[...] + mod_ref[0, gate_idx:gate_idx + 1, :] * acc

    _stream_tile(refs[:-3], lat_tiles, pl.program_id(1), finish)


def _proj_res(a, w, xres, mod, seg, *, rows, gate_idx, tm=TM_BIG, tn=TN):
    k, n = w.shape
    x_specs, x_args, lat_tiles = _stream_specs(xres, (tm, tn), lambda j, i: i, lambda j, i: j)
    return pl.pallas_call(
        functools.partial(_res_kernel, lat_tiles=lat_tiles, gate_idx=gate_idx),
        out_shape=jax.ShapeDtypeStruct((rows, n), F32),
        grid=(n // tn, rows // tm),
        in_specs=[pl.BlockSpec((tm, k), lambda j, i: (i, 0)),
                  pl.BlockSpec((k, tn), lambda j, i: (0, j))] + x_specs
                 + [pl.BlockSpec((1, 6, tn), lambda j, i: (seg(i, tm), 0, j))],
        out_specs=pl.BlockSpec((tm, tn), lambda j, i: (i, j)),
        scratch_shapes=[pltpu.VMEM((k, tn), BF16)],
        compiler_params=_cparams(2, 52),
        name="proj_res",
    )(a, w, *x_args, mod)


def _plain_kernel(a_ref, w_ref, o_ref, wb_scr):
    _cast_weight_tile(w_ref, wb_scr)
    o_ref[...] = _dot(a_ref[...], wb_scr[...]).astype(o_ref.dtype)


def _proj_plain(a, w, *, rows, tm=TM_BIG, tn=TN):
    k, n = w.shape
    return pl.pallas_call(
        _plain_kernel,
        out_shape=jax.ShapeDtypeStruct((rows, n), BF16),
        grid=(n // tn, rows // tm),
        in_specs=[pl.BlockSpec((tm, k), lambda j, i: (i, 0)),
                  pl.BlockSpec((k, tn), lambda j, i: (0, j))],
        out_specs=pl.BlockSpec((tm, tn), lambda j, i: (i, j)),
        scratch_shapes=[pltpu.VMEM((k, tn), BF16)],
        compiler_params=_cparams(2, 48),
        name="proj_plain",
    )(a, w)


def _ffn_down_kernel(a_ref, v_ref, ap_ref, an_ref, cw_ref, cb_ref, w_ref, x_ref, mod_ref, o_ref, g0_scr, g1_scr, *,
                     tm, n_tiles, sub_chunks, n_lat, n_ctx, lat_rows, gate_idx):
    r = pl.program_id(0)
    j = pl.program_id(1)

    @pl.when((r == 0) & (j == 0))
    def _():
        g1_scr[...] = jnp.zeros(g1_scr.shape, BF16)

    tile = jnp.minimum(r, n_tiles - 1)
    row = lax.broadcasted_iota(jnp.int32, (tm, LANES), 0)
    grow = tile * tm + row
    is_ctx = grow >= lat_rows
    first = ((grow & (n_ctx - 1)) == 0) & (is_ctx | ((grow & (n_lat - 1)) == 0))
    nxt = grow + 1
    last = ((nxt & (n_ctx - 1)) == 0) & (is_ctx | ((nxt & (n_lat - 1)) == 0))
    top = row == 0
    bot = row == tm - 1

    def step(g_write, g_read):
        for c in range(sub_chunks):
            sl = slice(c * LANES, (c + 1) * LANES)
            a = a_ref[:, sl].astype(F32)
            a_dn = jnp.where(top, ap_ref[7:8, sl].astype(F32), pltpu.roll(a, 1, 0))
            a_dn = jnp.where(first, 0.0, a_dn)
            a_up = jnp.where(bot, an_ref[0:1, sl].astype(F32), pltpu.roll(a, tm - 1, 0))
            a_up = jnp.where(last, 0.0, a_up)
            cv = cw_ref[0:1, sl] * a_dn + cw_ref[1:2, sl] * a + cw_ref[2:3, sl] * a_up + cb_ref[:, sl]
            dst = pl.ds(pl.multiple_of((j * sub_chunks + c) * LANES, LANES), LANES)
            g_write[:, dst] = (cv / (1.0 + jnp.exp(-cv)) * v_ref[:, sl].astype(F32)).astype(BF16)
        acc = _dot(g_read[...], w_ref[...])
        o_ref[...] = x_ref[...] + mod_ref[0, gate_idx:gate_idx + 1, :] * acc

    pl.when(r % 2 == 0)(lambda: step(g0_scr, g1_scr))
    pl.when(r % 2 == 1)(lambda: step(g1_scr, g0_scr))


def _ffn_down(u, conv_w, conv_b, w, xres, mod, seg, *, rows, n_lat, n_ctx, lat_rows, gate_idx, tm=TM, tn=TN):
    dff, n = w.shape
    nb8 = u.shape[0] // 8
    n_tiles = rows // tm
    n_j = n // tn
    ts = dff // n_j
    sub_chunks = ts // LANES
    assert sub_chunks * n_j * LANES == dff
    kern = functools.partial(_ffn_down_kernel, tm=tm, n_tiles=n_tiles, sub_chunks=sub_chunks, n_lat=n_lat,
                             n_ctx=n_ctx, lat_rows=lat_rows, gate_idx=gate_idx)

    def cur(r):
        return jnp.minimum(r, n_tiles - 1)

    def prev(r):
        return jnp.maximum(r - 1, 0)

    return pl.pallas_call(
        kern,
        out_shape=jax.ShapeDtypeStruct((rows, n), F32),
        grid=(n_tiles + 1, n_j),
        in_specs=[pl.BlockSpec((tm, ts), lambda r, j: (cur(r), j)),
                  pl.BlockSpec((tm, ts), lambda r, j: (cur(r), n_j + j)),
                  pl.BlockSpec((8, ts), lambda r, j: (jnp.maximum(cur(r) * (tm // 8) - 1, 0), j)),
                  pl.BlockSpec((8, ts), lambda r, j: (jnp.minimum((cur(r) + 1) * (tm // 8), nb8 - 1), j)),
                  pl.BlockSpec((3, ts), lambda r, j: (0, j)),
                  pl.BlockSpec((1, ts), lambda r, j: (0, j)),
                  pl.BlockSpec((dff, tn), lambda r, j: (0, j)),
                  pl.BlockSpec((tm, tn), lambda r, j: (prev(r), j)),
                  pl.BlockSpec((1, 6, tn), lambda r, j: (seg(prev(r), tm), 0, j))],
        out_specs=pl.BlockSpec((tm, tn), lambda r, j: (prev(r), jnp.where(r == 0, 0, j))),
        scratch_shapes=[pltpu.VMEM((tm, dff), BF16), pltpu.VMEM((tm, dff), BF16)],
        compiler_params=_cparams(2, 56),
        name="ffn_down",
    )(u, u, u, u, conv_w, conv_b.reshape(1, dff), w, xres, mod)


def _softmax_first(s, v, sink_tile):
    n = s.shape[1]
    if sink_tile is not None:
        s = jnp.concatenate([s, sink_tile], axis=1)
    m = jnp.max(s, axis=1, keepdims=True)
    p = jnp.exp(s - m)
    return m, jnp.sum(p, axis=1, keepdims=True), _dot(p[:, :n].astype(BF16), v)


def _softmax_empty(rows, dv):
    return jnp.full((rows, 1), NEG_INF, F32), jnp.zeros((rows, 1), F32), jnp.zeros((rows, dv), F32)


def _softmax_update(carry, s, v, sink_tile=None):
    m, l, acc = carry
    n = s.shape[1]
    if sink_tile is not None:
        s = jnp.concatenate([s, sink_tile], axis=1)
    m_new = jnp.maximum(m, jnp.max(s, axis=1, keepdims=True))
    p = jnp.exp(s - m_new)
    alpha = jnp.exp(m - m_new)
    return m_new, alpha * l + jnp.sum(p, axis=1, keepdims=True), alpha * acc + _dot(p[:, :n].astype(BF16), v)


def _attn_kernel(*refs, mode, r_heads, n_lat, tk, has_sink, with_ctx, rows, nt):
    refs = list(refs)
    q_ref, kc_ref, vc_ref, kl_ref, vl_ref = refs[:5]
    pos = 5
    sink_ref = bias_ref = None
    if has_sink:
        sink_ref = refs[pos]
        pos += 1
    if mode in ("window", "na"):
        bias_ref = refs[pos]
        pos += 1
    o_ref = refs[-1]
    t = pl.program_id(2)
    tq = q_ref.shape[1]
    dq = q_ref.shape[2]
    m_rows = r_heads * tq

    def prologue():
        q = q_ref[...].reshape(m_rows, dq)
        sink_tile = None
        if has_sink:
            sink_tile = jnp.concatenate([jnp.broadcast_to(sink_ref[r], (tq, LANES)) for r in range(r_heads)], axis=0)
        return q, sink_tile, _dot_nt(q, kc_ref[0])

    def finish(carry):
        _, l, acc = carry
        o = acc / l
        dv = acc.shape[-1]
        for r in range(r_heads):
            o_ref[:, r * dv:(r + 1) * dv] = o[r * tq:(r + 1) * tq].astype(o_ref.dtype)

    def latent_tile():
        q, sink_tile, s_ctx = prologue()
        if mode == "dense":
            n_chunks = n_lat // tk
            unroll = math.gcd(KV_UNROLL, n_chunks)

            def chunk(c):
                return pl.ds(pl.multiple_of(c * tk, tk), tk)

            carry = _softmax_first(s_ctx, vc_ref[0], sink_tile)

            def body(c, carry):
                scores = [_dot_nt(q, kl_ref[0, chunk(unroll * c + u), :]) for u in range(unroll)]
                for u in range(unroll):
                    carry = _softmax_update(carry, scores[u], vl_ref[0, chunk(unroll * c + u), :])
                return carry

            carry = lax.fori_loop(0, n_chunks // unroll, body, carry)
        else:
            variant = jnp.where(t == 0, 0, jnp.where(t == nt - 1, 2, 1))
            if mode == "window":
                span = tq + 2 * WINDOW
                start = pl.multiple_of(jnp.clip(t * tq - WINDOW, 0, n_lat - span), WINDOW)
                bias = jnp.concatenate([bias_ref[variant]] * r_heads, axis=0)
            else:
                span = NA_KEY_ROWS * GRID_W
                ks_row = jnp.clip(t * (tq // GRID_W) - NA_KH // 2, 0, rows - NA_KEY_ROWS)
                start = pl.multiple_of(ks_row * GRID_W, GRID_W)
                bias = bias_ref[variant].reshape(m_rows, span)
            sl = pl.ds(start, span)
            s_loc = _dot_nt(q, kl_ref[0, sl, :]) + bias
            s_all = jnp.concatenate([s_ctx, s_loc], axis=1)
            v_all = jnp.concatenate([vc_ref[0], vl_ref[0, sl, :]], axis=0)
            carry = _softmax_update(_softmax_empty(m_rows, v_all.shape[1]), s_all, v_all, sink_tile)
        finish(carry)

    def context_tile():
        _, sink_tile, s_ctx = prologue()
        finish(_softmax_first(s_ctx, vc_ref[0], sink_tile))

    if with_ctx:
        pl.when(t < nt)(latent_tile)
        pl.when(t == nt)(context_tile)
    else:
        latent_tile()


def _window_bias():
    span = TQ + 2 * WINDOW
    d = np.arange(TQ)[:, None] - np.arange(span)[None, :]
    offsets = (0, WINDOW, span - TQ)
    return np.stack([np.where(np.abs(d + o) <= WINDOW, 0.0, NEG_INF) for o in offsets]).astype(np.float32)


def _attention(qa, ka, va, *, mode, with_ctx, batch, n_lat, n_ctx, groups, r_heads, q_base, k_base, v_base,
               sink=None, bias=None, tq=TQ, tk=TK):
    dq = qa.shape[2]
    dv = va.shape[2]
    lat_rows = batch * n_lat
    nq = n_lat // tq
    ctx_blk = lat_rows // n_ctx
    tk = min(tk, n_lat)
    assert n_lat % tk == 0 and (not with_ctx or tq == n_ctx)

    def qrow(b, g, t):
        return jnp.where(t < nq, b * nq + t, lat_rows // tq + b) if with_ctx else b * nq + t

    in_specs = [pl.BlockSpec((r_heads, tq, dq), lambda b, g, t: (q_base + g, qrow(b, g, t), 0)),
                pl.BlockSpec((1, n_ctx, dq), lambda b, g, t: (k_base + g, ctx_blk + b, 0)),
                pl.BlockSpec((1, n_ctx, dv), lambda b, g, t: (v_base + g, ctx_blk + b, 0)),
                pl.BlockSpec((1, n_lat, dq), lambda b, g, t: (k_base + g, b, 0)),
                pl.BlockSpec((1, n_lat, dv), lambda b, g, t: (v_base + g, b, 0))]
    args = [qa, ka, va, ka, va]
    if sink is not None:
        in_specs.append(pl.BlockSpec((r_heads, 1, LANES), lambda b, g, t: (g, 0, 0)))
        args.append(jnp.full((sink.shape[0], 1, LANES), NEG_INF, F32).at[:, 0, 0].set(sink))
    if mode == "window":
        in_specs.append(pl.BlockSpec(bias.shape, lambda b, g, t: (0, 0, 0)))
        args.append(bias)
    if mode == "na":
        in_specs.append(pl.BlockSpec((3, r_heads, tq, NA_KEY_ROWS * GRID_W), lambda b, g, t: (0, g, 0, 0)))
        args.append(bias)
    kern = functools.partial(_attn_kernel, mode=mode, r_heads=r_heads, n_lat=n_lat, tk=tk, has_sink=sink is not None,
                             with_ctx=with_ctx, rows=n_lat // GRID_W, nt=nq)
    out_rows = lat_rows + (batch * n_ctx if with_ctx else 0)
    return pl.pallas_call(
        kern,
        out_shape=jax.ShapeDtypeStruct((out_rows, groups * r_heads * dv), BF16),
        grid=(batch, groups, nq + (1 if with_ctx else 0)),
        in_specs=in_specs,
        out_specs=pl.BlockSpec((tq, r_heads * dv), lambda b, g, t: (qrow(b, g, t), g)),
        compiler_params=_cparams(3, 48),
        name="attn_" + mode,
    )(*args)


def _na_variants(rows, q_rows):
    nt = rows // q_rows

    def structure(t):
        ks_row = int(np.clip(t * q_rows - NA_KH // 2, 0, rows - NA_KEY_ROWS))
        out = []
        for qr in range(q_rows):
            r = t * q_rows + qr
            rs = int(np.clip(r - NA_KH // 2, 0, rows - NA_KH))
            out.append([(rs <= ks_row + j < rs + NA_KH, ks_row + j - r + NA_KH - 1) for j in range(NA_KEY_ROWS)])
        return out

    reps = [structure(0), structure(1), structure(nt - 1)]
    for t in range(nt):
        want = reps[0] if t == 0 else reps[2] if t == nt - 1 else reps[1]
        got = structure(t)
        for wq, gq in zip(want, got):
            for (wv, wa), (gv, ga) in zip(wq, gq):
                assert wv == gv and (not wv or wa == ga), "neighbourhood tile variants do not cover this grid"
    return reps


def _na_bias_kernel(rb_ref, o_ref, *, variants):
    c = lax.broadcasted_iota(jnp.int32, (GRID_W, LANES), 0)
    lane = lax.broadcasted_iota(jnp.int32, (GRID_W, LANES), 1)
    kc = lane & (GRID_W - 1)
    cs = jnp.clip(c - NA_KW // 2, 0, GRID_W - NA_KW)
    col_ok = (kc >= cs) & (kc < cs + NA_KW)
    low = lane < GRID_W
    neg = jnp.full((GRID_W, LANES), NEG_INF, F32)
    for v, var in enumerate(variants):
        for qr, krows in enumerate(var):
            for jp in range(NA_KEY_ROWS // 2):
                halves = []
                for hf in range(2):
                    valid, a = krows[2 * jp + hf]
                    if valid:
                        x = jnp.broadcast_to(rb_ref[0, a:a + 1, :], (GRID_W, LANES))
                        shift = (LANES - (NA_KW - 1) + GRID_W * hf) % LANES
                        halves.append(pltpu.roll(x, shift, 1, stride=1, stride_axis=0))
                    else:
                        halves.append(neg)
                tile = jnp.where(col_ok, jnp.where(low, halves[0], halves[1]), NEG_INF)
                o_ref[v, 0, qr * GRID_W:(qr + 1) * GRID_W, jp * LANES:(jp + 1) * LANES] = tile


def _na_bias(rel_bias, rows):
    q_rows = TQ // GRID_W
    variants = _na_variants(rows, q_rows)
    nh, na, nb = rel_bias.shape
    rb = jnp.zeros((nh, 16, LANES), F32).at[:, :na, :nb].set(rel_bias)
    span = NA_KEY_ROWS * GRID_W
    return pl.pallas_call(
        functools.partial(_na_bias_kernel, variants=variants),
        out_shape=jax.ShapeDtypeStruct((3, nh, TQ, span), F32),
        grid=(nh,),
        in_specs=[pl.BlockSpec((1, 16, LANES), lambda h: (h, 0, 0))],
        out_specs=pl.BlockSpec((3, 1, TQ, span), lambda h: (0, h, 0, 0)),
        compiler_params=_cparams(1, 32),
        name="na_bias",
    )(rb)


def _mla_dq_kernel(a_ref, w_ref, g_ref, o_ref):
    o_ref[...] = _rms(_dot(a_ref[...], w_ref[...]), g_ref[...]).astype(BF16)


def _mla_dq(h, w, g, tm=TM):
    t, d = h.shape
    n = w.shape[1]
    return pl.pallas_call(
        _mla_dq_kernel,
        out_shape=jax.ShapeDtypeStruct((t, n), BF16),
        grid=(t // tm,),
        in_specs=[pl.BlockSpec((tm, d), lambda i: (i, 0)),
                  pl.BlockSpec((d, n), lambda i: (0, 0)),
                  pl.BlockSpec((1, n), lambda i: (0, 0))],
        out_specs=pl.BlockSpec((tm, n), lambda i: (i, 0)),
        compiler_params=_cparams(1, 48),
        name="mla_dq",
    )(h, w, g.reshape(1, n))


def _mla_uq_kernel(a_ref, w_ref, cos_ref, s1_ref, s2_ref, o_ref, *, hpt, scale):
    acc = _dot(a_ref[...], w_ref[...])
    for r in range(hpt):
        base = r * MLA_QK
        o_ref[r, :, :MLA_NOPE] = (acc[:, base:base + MLA_NOPE] * scale).astype(BF16)
        pe = _rope64(acc[:, base + MLA_NOPE:base + MLA_QK], cos_ref[...], s1_ref[...], s2_ref[...])
        o_ref[r, :, MLA_NOPE:] = (pe * scale).astype(BF16)


def _mla_uq(cq, w, tabs, *, scale, tm=TM_BIG, tn=TN):
    t, k = cq.shape
    n = w.shape[1]
    hpt = tn // MLA_QK
    cos, s1, s2, tab_idx = tabs
    tab_spec = pl.BlockSpec((tm, LANES), lambda i, j: (tab_idx(i, tm), 0))
    return pl.pallas_call(
        functools.partial(_mla_uq_kernel, hpt=hpt, scale=scale),
        out_shape=jax.ShapeDtypeStruct((n // MLA_QK, t, MLA_QK), BF16),
        grid=(t // tm, n // tn),
        in_specs=[pl.BlockSpec((tm, k), lambda i, j: (i, 0)),
                  pl.BlockSpec((k, tn), lambda i, j: (0, j)),
                  tab_spec, tab_spec, tab_spec],
        out_specs=pl.BlockSpec((hpt, tm, MLA_QK), lambda i, j: (j, i, 0)),
        compiler_params=_cparams(2, 40),
        name="mla_uq",
    )(cq, w, cos, s1, s2)


def _mla_dkv_kernel(a_ref, w_ref, g_ref, cos_ref, s1_ref, s2_ref, ckv_ref, kpe_ref, *, rank):
    acc = _dot(a_ref[...], w_ref[...])
    ckv_ref[...] = _rms(acc[:, :rank], g_ref[...]).astype(BF16)
    kpe_ref[...] = _rope64(acc[:, rank:], cos_ref[...], s1_ref[...], s2_ref[...]).astype(BF16)


def _mla_dkv(h, w, g, tabs, *, rank, tm=TM):
    t, d = h.shape
    n = w.shape[1]
    cos, s1, s2, tab_idx = tabs
    tab_spec = pl.BlockSpec((tm, LANES), lambda i: (tab_idx(i, tm), 0))
    return pl.pallas_call(
        functools.partial(_mla_dkv_kernel, rank=rank),
        out_shape=(jax.ShapeDtypeStruct((t, rank), BF16), jax.ShapeDtypeStruct((t, LANES), BF16)),
        grid=(t // tm,),
        in_specs=[pl.BlockSpec((tm, d), lambda i: (i, 0)),
                  pl.BlockSpec((d, n), lambda i: (0, 0)),
                  pl.BlockSpec((1, rank), lambda i: (0, 0)),
                  tab_spec, tab_spec, tab_spec],
        out_specs=(pl.BlockSpec((tm, rank), lambda i: (i, 0)),
                   pl.BlockSpec((tm, LANES), lambda i: (i, 0))),
        compiler_params=_cparams(1, 40),
        name="mla_dkv",
    )(h, w, g.reshape(1, rank), cos, s1, s2)


def _mla_ukv_kernel(a_ref, w_ref, kpe_ref, k_ref, v_ref, *, hpt):
    acc = _dot(a_ref[...], w_ref[...])
    for r in range(hpt):
        base = r * (MLA_NOPE + MLA_V)
        k_ref[r, :, :MLA_NOPE] = acc[:, base:base + MLA_NOPE].astype(BF16)
        k_ref[r, :, MLA_NOPE:] = kpe_ref[...]
        v_ref[r] = acc[:, base + MLA_NOPE:base + MLA_NOPE + MLA_V].astype(BF16)


def _mla_ukv(ckv, w, kpe, *, tm=TM_BIG, tn=TN):
    t, k = ckv.shape
    n = w.shape[1]
    hpt = tn // (MLA_NOPE + MLA_V)
    nh = n // (MLA_NOPE + MLA_V)
    return pl.pallas_call(
        functools.partial(_mla_ukv_kernel, hpt=hpt),
        out_shape=(jax.ShapeDtypeStruct((nh, t, MLA_QK), BF16), jax.ShapeDtypeStruct((nh, t, MLA_V), BF16)),
        grid=(t // tm, n // tn),
        in_specs=[pl.BlockSpec((tm, k), lambda i, j: (i, 0)),
                  pl.BlockSpec((k, tn), lambda i, j: (0, j)),
                  pl.BlockSpec((tm, LANES), lambda i, j: (i, 0))],
        out_specs=(pl.BlockSpec((hpt, tm, MLA_QK), lambda i, j: (j, i, 0)),
                   pl.BlockSpec((hpt, tm, MLA_V), lambda i, j: (j, i, 0))),
        compiler_params=_cparams(2, 40),
        name="mla_ukv",
    )(ckv, w, kpe)


def _axial_angles(n_tok, rot_dim):
    n_freq = rot_dim // 4
    freqs = ROPE_THETA ** (-jnp.arange(n_freq, dtype=F32) / n_freq)
    t = jnp.arange(n_tok)
    row = (t // GRID_W).astype(F32)
    col = (t % GRID_W).astype(F32)
    return jnp.concatenate([row[:, None] * freqs, col[:, None] * freqs], axis=-1)


def _rope_tables(n_lat, tail):
    ang = _axial_angles(n_lat, HEAD_DIM)
    cos, sin = jnp.cos(ang), jnp.sin(ang)
    cos_t = jnp.concatenate([cos, cos], axis=-1)
    sin_t = jnp.concatenate([-sin, sin], axis=-1)
    cos_t = jnp.concatenate([cos_t, jnp.ones((tail, HEAD_DIM), F32)], axis=0)
    sin_t = jnp.concatenate([sin_t, jnp.zeros((tail, HEAD_DIM), F32)], axis=0)
    return cos_t, sin_t


def _rope64_tables(n_lat, tail):
    ang = _axial_angles(n_lat, MLA_ROPE)
    cos, sin = jnp.cos(ang), jnp.sin(ang)
    half = MLA_ROPE // 2
    z = jnp.zeros((n_lat, half), F32)
    zpad = jnp.zeros((n_lat, LANES - MLA_ROPE), F32)
    cos_t = jnp.concatenate([cos, cos, zpad], axis=-1)
    s1 = jnp.concatenate([z, sin, zpad], axis=-1)
    s2 = jnp.concatenate([-sin, z, zpad], axis=-1)
    ident = jnp.concatenate([jnp.ones((tail, MLA_ROPE), F32), jnp.zeros((tail, LANES - MLA_ROPE), F32)], axis=-1)
    zt = jnp.zeros((tail, LANES), F32)
    return (jnp.concatenate([cos_t, ident], axis=0), jnp.concatenate([s1, zt], axis=0),
            jnp.concatenate([s2, zt], axis=0))


def kernel(x, c, ctx, c_ctx, l0_ada_w, l0_ada_b, l0_attn_norm_g, l0_w_qkv, l0_q_norm_g, l0_k_norm_g, l0_w_o, l0_ffn_norm_g, l0_ffn_w_up, l0_ffn_conv_w, l0_ffn_conv_b, l0_ffn_w_down, l1_ada_w, l1_ada_b, l1_attn_norm_g, l1_w_qkv, l1_sink, l1_w_o, l1_ffn_norm_g, l1_ffn_w_up, l1_ffn_conv_w, l1_ffn_conv_b, l1_ffn_w_down, l2_ada_w, l2_ada_b, l2_attn_norm_g, l2_w_qkv, l2_rel_bias, l2_w_o, l2_ffn_norm_g, l2_ffn_w_up, l2_ffn_conv_w, l2_ffn_conv_b, l2_ffn_w_down, l3_ada_w, l3_ada_b, l3_attn_norm_g, l3_w_dq, l3_q_norm_g, l3_w_uq, l3_w_dkv, l3_kv_norm_g, l3_w_ukv, l3_w_o, l3_ffn_norm_g, l3_ffn_w_up, l3_ffn_conv_w, l3_ffn_conv_b, l3_ffn_w_down, final_norm_g):
    batch, n_lat, d = x.shape
    n_ctx = ctx.shape[1]
    lat_rows = batch * n_lat
    t_rows = lat_rows + batch * n_ctx
    rows = n_lat // GRID_W
    assert batch + 1 <= 8 and n_ctx == TQ and n_lat % TM_BIG == 0 and (batch * n_ctx) % TM_BIG == 0
    assert n_lat & (n_lat - 1) == 0 and rows >= NA_KEY_ROWS and n_lat >= TQ + 2 * WINDOW

    def seg(i, tile):
        return jnp.minimum(i * tile // n_lat, batch)

    def tab_idx(i, tile):
        return jnp.where(i < lat_rows // tile, i % (n_lat // tile), n_lat // tile)

    cos_t, sin_t = _rope_tables(n_lat, TM_BIG)
    tabs128 = (cos_t, sin_t, tab_idx)
    tabs64 = _rope64_tables(n_lat, TM_BIG) + (tab_idx,)
    ones_g = jnp.ones((HEAD_DIM,), F32)

    xa = (x.reshape(lat_rows, d), ctx.reshape(batch * n_ctx, d))
    c8 = jnp.zeros((8, d), F32).at[:batch].set(c).at[batch].set(c_ctx)

    ada = [(l0_ada_w, l0_ada_b), (l1_ada_w, l1_ada_b), (l2_ada_w, l2_ada_b), (l3_ada_w, l3_ada_b)]
    attn_g = [l0_attn_norm_g, l1_attn_norm_g, l2_attn_norm_g, l3_attn_norm_g]
    ffn_g = [l0_ffn_norm_g, l1_ffn_norm_g, l2_ffn_norm_g, l3_ffn_norm_g]
    w_o = [l0_w_o, l1_w_o, l2_w_o, l3_w_o]
    ffn_p = [(l0_ffn_w_up, l0_ffn_conv_w, l0_ffn_conv_b, l0_ffn_w_down),
             (l1_ffn_w_up, l1_ffn_conv_w, l1_ffn_conv_b, l1_ffn_w_down),
             (l2_ffn_w_up, l2_ffn_conv_w, l2_ffn_conv_b, l2_ffn_w_down),
             (l3_ffn_w_up, l3_ffn_conv_w, l3_ffn_conv_b, l3_ffn_w_down)]
    w_qkv = [l0_w_qkv, l1_w_qkv, l2_w_qkv]
    gqa_scale = HEAD_DIM ** -0.5
    common = dict(batch=batch, n_lat=n_lat, n_ctx=n_ctx)
    gqa = dict(groups=N_KV_HEADS, r_heads=Q_PER_KV, q_base=0, k_base=N_HEADS, v_base=N_HEADS + N_KV_HEADS,
               **common)
    depth = 4

    for layer in range(depth):
        need_ctx = layer < depth - 1
        mod = _ada(c8, *ada[layer]).reshape(8, 6, d)
        h = _normmod(xa, attn_g[layer], mod, seg, rows=t_rows, shift_idx=0, scale_idx=1)

        if layer < 3:
            w = w_qkv[layer]
            if layer == 0:
                qkv = _qkv_proj(h, w, tabs128, l0_q_norm_g, l0_k_norm_g, use_norm=True, use_rope=True,
                                scale=gqa_scale)
            elif layer == 1:
                qkv = _qkv_proj(h, w, tabs128, ones_g, ones_g, use_norm=False, use_rope=True, scale=gqa_scale)
            else:
                qkv = _qkv_proj(h, w, tabs128, ones_g, ones_g, use_norm=False, use_rope=False, scale=gqa_scale)
            if layer == 0:
                o = _attention(qkv, qkv, qkv, mode="dense", with_ctx=True, **gqa)
            elif layer == 1:
                o = _attention(qkv, qkv, qkv, mode="window", with_ctx=True, sink=l1_sink,
                               bias=jnp.asarray(_window_bias()), **gqa)
            else:
                o = _attention(qkv, qkv, qkv, mode="na", with_ctx=True, bias=_na_bias(l2_rel_bias, rows), **gqa)
        else:
            mla_scale = (MLA_NOPE + MLA_ROPE) ** -0.5
            rq = l3_w_uq.shape[0]
            w_uq = l3_w_uq.reshape(rq, N_HEADS, MLA_NOPE + MLA_ROPE)
            w_uq = jnp.pad(w_uq, ((0, 0), (0, 0), (0, MLA_QK - MLA_NOPE - MLA_ROPE))).reshape(rq, N_HEADS * MLA_QK)
            rank = l3_kv_norm_g.shape[0]
            w_dkv = jnp.pad(l3_w_dkv, ((0, 0), (0, LANES - MLA_ROPE)))
            cq = _mla_dq(h, l3_w_dq.astype(BF16), l3_q_norm_g)
            qh = _mla_uq(cq, w_uq.astype(BF16), tabs64, scale=mla_scale)
            ckv, kpe = _mla_dkv(h, w_dkv.astype(BF16), l3_kv_norm_g, tabs64, rank=rank)
            kh, vh = _mla_ukv(ckv, l3_w_ukv.astype(BF16), kpe)
            o = _attention(qh, kh, vh, mode="dense", with_ctx=False, groups=N_HEADS, r_heads=1, q_base=0, k_base=0,
                           v_base=0, tq=4 * TQ, tk=2 * TK, **common)

        out_rows = t_rows if need_ctx else lat_rows
        xa = _proj_res(o, w_o[layer], xa, mod, seg, rows=out_rows, gate_idx=2)
        h2 = _normmod(xa, ffn_g[layer], mod, seg, rows=out_rows, shift_idx=3, scale_idx=4)
        w_up, conv_w, conv_b, w_down = ffn_p[layer]
        u = _proj_plain(h2, w_up, rows=out_rows)
        xa = _ffn_down(u, conv_w, conv_b, w_down.astype(BF16), xa, mod, seg, rows=out_rows, n_lat=n_lat,
                       n_ctx=n_ctx, lat_rows=lat_rows, gate_idx=5)

    return _final_norm(xa, final_norm_g, rows=lat_rows).reshape(batch, n_lat, d)
```

```python
import functools
import math

import jax
import jax.numpy as jnp
import numpy as np
from jax import lax
from jax.experimental import pallas as pl
from jax.experimental.pallas import tpu as pltpu

F32 = jnp.float32
BF16 = jnp.bfloat16

GRID_W = 64
HEAD_DIM = 128
N_HEADS = 32
N_KV_HEADS = 8
Q_PER_KV = 4
WINDOW = 128
NA_KH = 8
NA_KW = 16
NA_KEY_ROWS = 12
MLA_NOPE = 128
MLA_ROPE = 64
MLA_V = 128
MLA_QK = 256
ROPE_THETA = 10000.0
EPS = 1e-6
NEG_INF = -1e30
LANES = 128
TQ = 256
TK = 1024
KV_UNROLL = 8
TM_BIG = 1024
TM = 512
TM_NORM = 256
TN = 512
MIB = 1024 * 1024


def _cparams(n_axes, vmem_mib):
    return pltpu.CompilerParams(dimension_semantics=("arbitrary",) * n_axes,
                                vmem_limit_bytes=vmem_mib * MIB)


def _dot(a, b):
    return jnp.dot(a, b, preferred_element_type=F32)


def _dot_nt(a, b):
    return lax.dot_general(a, b, (((1,), (1,)), ((), ())), preferred_element_type=F32)


def _rms(y, g):
    return y * lax.rsqrt(jnp.mean(y * y, axis=-1, keepdims=True) + EPS) * g


def _ada_kernel(c_ref, w_ref, b_ref, o_ref):
    x = c_ref[...]
    s = x / (1.0 + jnp.exp(-x))
    o_ref[...] = _dot(s.astype(BF16), w_ref[...].astype(BF16)) + b_ref[...]


def _ada(c8, w, b, tn=TN):
    d, n = w.shape
    return pl.pallas_call(
        _ada_kernel,
        out_shape=jax.ShapeDtypeStruct((8, n), F32),
        grid=(n // tn,),
        in_specs=[pl.BlockSpec((8, d), lambda j: (0, 0)),
                  pl.BlockSpec((d, tn), lambda j: (0, j)),
                  pl.BlockSpec((1, tn), lambda j: (0, j))],
        out_specs=pl.BlockSpec((8, tn), lambda j: (0, j)),
        compiler_params=_cparams(1, 40),
        name="ada",
    )(c8, w, b.reshape(1, n))


def _stream_specs(x, block, row_tile, col_tile):
    if not isinstance(x, tuple):
        return [pl.BlockSpec(block, lambda *ids: (row_tile(*ids), col_tile(*ids)))], [x], None
    lat_tiles = x[0].shape[0] // block[0]
    specs = [pl.BlockSpec(block, lambda *ids: (jnp.minimum(row_tile(*ids), lat_tiles - 1), col_tile(*ids))),
             pl.BlockSpec(block, lambda *ids: (jnp.maximum(row_tile(*ids) - lat_tiles, 0), col_tile(*ids)))]
    return specs, list(x), lat_tiles


def _stream_tile(refs, lat_tiles, row_tile, fn):
    if lat_tiles is None:
        fn(refs[0])
    else:
        pl.when(row_tile < lat_tiles)(lambda: fn(refs[0]))
        pl.when(row_tile >= lat_tiles)(lambda: fn(refs[1]))


def _normmod_kernel(*refs, lat_tiles, shift_idx, scale_idx):
    g_ref, mod_ref, o_ref = refs[-3:]

    def compute(x_ref):
        y = _rms(x_ref[...], g_ref[...])
        y = y * (1.0 + mod_ref[0, scale_idx:scale_idx + 1, :]) + mod_ref[0, shift_idx:shift_idx + 1, :]
        o_ref[...] = y.astype(o_ref.dtype)

    _stream_tile(refs[:-3], lat_tiles, pl.program_id(0), compute)


def _normmod(x, g, mod, seg, *, rows, shift_idx, scale_idx, tm=TM_NORM):
    d = g.shape[0]
    x_specs, x_args, lat_tiles = _stream_specs(x, (tm, d), lambda i: i, lambda i: 0)
    return pl.pallas_call(
        functools.partial(_normmod_kernel, lat_tiles=lat_tiles, shift_idx=shift_idx, scale_idx=scale_idx),
        out_shape=jax.ShapeDtypeStruct((rows, d), BF16),
        grid=(rows // tm,),
        in_specs=x_specs + [pl.BlockSpec((1, d), lambda i: (0, 0)),
                            pl.BlockSpec((1, 6, d), lambda i: (seg(i, tm), 0, 0))],
        out_specs=pl.BlockSpec((tm, d), lambda i: (i, 0)),
        compiler_params=_cparams(1, 40),
        name="normmod",
    )(*x_args, g.reshape(1, d), mod)


def _final_norm_kernel(x_ref, g_ref, o_ref):
    o_ref[...] = _rms(x_ref[...], g_ref[...])


def _final_norm(x, g, *, rows, tm=TM_NORM):
    d = x.shape[1]
    return pl.pallas_call(
        _final_norm_kernel,
        out_shape=jax.ShapeDtypeStruct((rows, d), F32),
        grid=(rows // tm,),
        in_specs=[pl.BlockSpec((tm, d), lambda i: (i, 0)),
                  pl.BlockSpec((1, d), lambda i: (0, 0))],
        out_specs=pl.BlockSpec((tm, d), lambda i: (i, 0)),
        compiler_params=_cparams(1, 32),
        name="final_norm",
    )(x, g.reshape(1, d))


def _rope128(y, cos, sin):
    return y * cos + pltpu.roll(y, HEAD_DIM // 2, 1) * sin


def _rope64(y, cos, s1, s2):
    half = MLA_ROPE // 2
    return y * cos + pltpu.roll(y, half, 1) * s1 + pltpu.roll(y, LANES - half, 1) * s2


def _cast_weight_tile(w_ref, wb_scr):
    @pl.when(pl.program_id(1) == 0)
    def _():
        wb_scr[...] = w_ref[...].astype(BF16)


def _qkv_kernel(a_ref, w_ref, cos_ref, sin_ref, gq_ref, gk_ref, o_ref, wb_scr, *,
                nq_tiles, nqk_tiles, hpt, use_norm, use_rope, scale):
    j = pl.program_id(0)
    _cast_weight_tile(w_ref, wb_scr)
    acc = _dot(a_ref[...], wb_scr[...])

    @pl.when(j < nqk_tiles)
    def _():
        is_q = j < nq_tiles
        fac = jnp.where(is_q, scale, 1.0).astype(F32)
        g = jnp.where(is_q, gq_ref[...], gk_ref[...])
        for r in range(hpt):
            y = acc[:, r * HEAD_DIM:(r + 1) * HEAD_DIM]
            if use_norm:
                y = _rms(y, g)
            if use_rope:
                y = _rope128(y, cos_ref[...], sin_ref[...])
            o_ref[r] = (y * fac).astype(BF16)

    @pl.when(j >= nqk_tiles)
    def _():
        for r in range(hpt):
            o_ref[r] = acc[:, r * HEAD_DIM:(r + 1) * HEAD_DIM].astype(BF16)


def _qkv_proj(h, w, tabs, gq, gk, *, use_norm, use_rope, scale, tm=TM_BIG, tn=TN):
    t, d = h.shape
    n = w.shape[1]
    hpt = tn // HEAD_DIM
    cos, sin, tab_idx = tabs
    kern = functools.partial(
        _qkv_kernel, nq_tiles=N_HEADS * HEAD_DIM // tn, nqk_tiles=(N_HEADS + N_KV_HEADS) * HEAD_DIM // tn,
        hpt=hpt, use_norm=use_norm, use_rope=use_rope, scale=scale)
    return pl.pallas_call(
        kern,
        out_shape=jax.ShapeDtypeStruct((n // HEAD_DIM, t, HEAD_DIM), BF16),
        grid=(n // tn, t // tm),
        in_specs=[pl.BlockSpec((tm, d), lambda j, i: (i, 0)),
                  pl.BlockSpec((d, tn), lambda j, i: (0, j)),
                  pl.BlockSpec((tm, HEAD_DIM), lambda j, i: (tab_idx(i, tm), 0)),
                  pl.BlockSpec((tm, HEAD_DIM), lambda j, i: (tab_idx(i, tm), 0)),
                  pl.BlockSpec((1, HEAD_DIM), lambda j, i: (0, 0)),
                  pl.BlockSpec((1, HEAD_DIM), lambda j, i: (0, 0))],
        out_specs=pl.BlockSpec((hpt, tm, HEAD_DIM), lambda j, i: (j, i, 0)),
        scratch_shapes=[pltpu.VMEM((d, tn), BF16)],
        compiler_params=_cparams(2, 48),
        name="qkv",
    )(h, w, cos, sin, gq.reshape(1, HEAD_DIM), gk.reshape(1, HEAD_DIM))


def _res_kernel(a_ref, w_ref, *refs, lat_tiles, gate_idx):
    mod_ref, o_ref, wb_scr = refs[-3:]
    _cast_weight_tile(w_ref, wb_scr)
    acc = _dot(a_ref[...], wb_scr[...])

    def finish(x_ref):
        o_ref[...] = x_ref[...] + mod_ref[0, gate_idx:gate_idx + 1, :] * acc

    _stream_tile(refs[:-3], lat_tiles, pl.program_id(1), finish)


def _proj_res(a, w, xres, mod, seg, *, rows, gate_idx, tm=TM_BIG, tn=TN):
    k, n = w.shape
    x_specs, x_args, lat_tiles = _stream_specs(xres, (tm, tn), lambda j, i: i, lambda j, i: j)
    return pl.pallas_call(
        functools.partial(_res_kernel, lat_tiles=lat_tiles, gate_idx=gate_idx),
        out_shape=jax.ShapeDtypeStruct((rows, n), F32),
        grid=(n // tn, rows // tm),
        in_specs=[pl.BlockSpec((tm, k), lambda j, i: (i, 0)),
                  pl.BlockSpec((k, tn), lambda j, i: (0, j))] + x_specs
                 + [pl.BlockSpec((1, 6, tn), lambda j, i: (seg(i, tm), 0, j))],
        out_specs=pl.BlockSpec((tm, tn), lambda j, i: (i, j)),
        scratch_shapes=[pltpu.VMEM((k, tn), BF16)],
        compiler_params=_cparams(2, 52),
        name="proj_res",
    )(a, w, *x_args, mod)


def _plain_kernel(a_ref, w_ref, o_ref, wb_scr):
    _cast_weight_tile(w_ref, wb_scr)
    o_ref[...] = _dot(a_ref[...], wb_scr[...]).astype(o_ref.dtype)


def _proj_plain(a, w, *, rows, tm=TM_BIG, tn=TN):
    k, n = w.shape
    return pl.pallas_call(
        _plain_kernel,
        out_shape=jax.ShapeDtypeStruct((rows, n), BF16),
        grid=(n // tn, rows // tm),
        in_specs=[pl.BlockSpec((tm, k), lambda j, i: (i, 0)),
                  pl.BlockSpec((k, tn), lambda j, i: (0, j))],
        out_specs=pl.BlockSpec((tm, tn), lambda j, i: (i, j)),
        scratch_shapes=[pltpu.VMEM((k, tn), BF16)],
        compiler_params=_cparams(2, 48),
        name="proj_plain",
    )(a, w)


def _ffn_down_kernel(a_ref, v_ref, ap_ref, an_ref, cw_ref, cb_ref, w_ref, x_ref, mod_ref, o_ref, g0_scr, g1_scr, *,
                     tm, n_tiles, sub_chunks, n_lat, n_ctx, lat_rows, gate_idx):
    r = pl.program_id(0)
    j = pl.program_id(1)

    @pl.when((r == 0) & (j == 0))
    def _():
        g1_scr[...] = jnp.zeros(g1_scr.shape, BF16)

    tile = jnp.minimum(r, n_tiles - 1)
    row = lax.broadcasted_iota(jnp.int32, (tm, LANES), 0)
    grow = tile * tm + row
    is_ctx = grow >= lat_rows
    first = ((grow & (n_ctx - 1)) == 0) & (is_ctx | ((grow & (n_lat - 1)) == 0))
    nxt = grow + 1
    last = ((nxt & (n_ctx - 1)) == 0) & (is_ctx | ((nxt & (n_lat - 1)) == 0))
    top = row == 0
    bot = row == tm - 1

    def step(g_write, g_read):
        for c in range(sub_chunks):
            sl = slice(c * LANES, (c + 1) * LANES)
            a = a_ref[:, sl].astype(F32)
            a_dn = jnp.where(top, ap_ref[7:8, sl].astype(F32), pltpu.roll(a, 1, 0))
            a_dn = jnp.where(first, 0.0, a_dn)
            a_up = jnp.where(bot, an_ref[0:1, sl].astype(F32), pltpu.roll(a, tm - 1, 0))
            a_up = jnp.where(last, 0.0, a_up)
            cv = cw_ref[0:1, sl] * a_dn + cw_ref[1:2, sl] * a + cw_ref[2:3, sl] * a_up + cb_ref[:, sl]
            dst = pl.ds(pl.multiple_of((j * sub_chunks + c) * LANES, LANES), LANES)
            g_write[:, dst] = (cv / (1.0 + jnp.exp(-cv)) * v_ref[:, sl].astype(F32)).astype(BF16)
        acc = _dot(g_read[...], w_ref[...])
        o_ref[...] = x_ref[...] + mod_ref[0, gate_idx:gate_idx + 1, :] * acc

    pl.when(r % 2 == 0)(lambda: step(g0_scr, g1_scr))
    pl.when(r % 2 == 1)(lambda: step(g1_scr, g0_scr))


def _ffn_down(u, conv_w, conv_b, w, xres, mod, seg, *, rows, n_lat, n_ctx, lat_rows, gate_idx, tm=TM, tn=TN):
    dff, n = w.shape
    nb8 = u.shape[0] // 8
    n_tiles = rows // tm
    n_j = n // tn
    ts = dff // n_j
    sub_chunks = ts // LANES
    assert sub_chunks * n_j * LANES == dff
    kern = functools.partial(_ffn_down_kernel, tm=tm, n_tiles=n_tiles, sub_chunks=sub_chunks, n_lat=n_lat,
                             n_ctx=n_ctx, lat_rows=lat_rows, gate_idx=gate_idx)

    def cur(r):
        return jnp.minimum(r, n_tiles - 1)

    def prev(r):
        return jnp.maximum(r - 1, 0)

    return pl.pallas_call(
        kern,
        out_shape=jax.ShapeDtypeStruct((rows, n), F32),
        grid=(n_tiles + 1, n_j),
        in_specs=[pl.BlockSpec((tm, ts), lambda r, j: (cur(r), j)),
                  pl.BlockSpec((tm, ts), lambda r, j: (cur(r), n_j + j)),
                  pl.BlockSpec((8, ts), lambda r, j: (jnp.maximum(cur(r) * (tm // 8) - 1, 0), j)),
                  pl.BlockSpec((8, ts), lambda r, j: (jnp.minimum((cur(r) + 1) * (tm // 8), nb8 - 1), j)),
                  pl.BlockSpec((3, ts), lambda r, j: (0, j)),
                  pl.BlockSpec((1, ts), lambda r, j: (0, j)),
                  pl.BlockSpec((dff, tn), lambda r, j: (0, j)),
                  pl.BlockSpec((tm, tn), lambda r, j: (prev(r), j)),
                  pl.BlockSpec((1, 6, tn), lambda r, j: (seg(prev(r), tm), 0, j))],
        out_specs=pl.BlockSpec((tm, tn), lambda r, j: (prev(r), jnp.where(r == 0, 0, j))),
        scratch_shapes=[pltpu.VMEM((tm, dff), BF16), pltpu.VMEM((tm, dff), BF16)],
        compiler_params=_cparams(2, 56),
        name="ffn_down",
    )(u, u, u, u, conv_w, conv_b.reshape(1, dff), w, xres, mod)


def _softmax_first(s, v, sink_tile):
    n = s.shape[1]
    if sink_tile is not None:
        s = jnp.concatenate([s, sink_tile], axis=1)
    m = jnp.max(s, axis=1, keepdims=True)
    p = jnp.exp(s - m)
    return m, jnp.sum(p, axis=1, keepdims=True), _dot(p[:, :n].astype(BF16), v)


def _softmax_empty(rows, dv):
    return jnp.full((rows, 1), NEG_INF, F32), jnp.zeros((rows, 1), F32), jnp.zeros((rows, dv), F32)


def _softmax_update(carry, s, v, sink_tile=None):
    m, l, acc = carry
    n = s.shape[1]
    if sink_tile is not None:
        s = jnp.concatenate([s, sink_tile], axis=1)
    m_new = jnp.maximum(m, jnp.max(s, axis=1, keepdims=True))
    p = jnp.exp(s - m_new)
    alpha = jnp.exp(m - m_new)
    return m_new, alpha * l + jnp.sum(p, axis=1, keepdims=True), alpha * acc + _dot(p[:, :n].astype(BF16), v)


def _attn_kernel(*refs, mode, r_heads, n_lat, tk, has_sink, with_ctx, rows, nt):
    refs = list(refs)
    q_ref, kc_ref, vc_ref, kl_ref, vl_ref = refs[:5]
    pos = 5
    sink_ref = bias_ref = None
    if has_sink:
        sink_ref = refs[pos]
        pos += 1
    if mode in ("window", "na"):
        bias_ref = refs[pos]
        pos += 1
    o_ref = refs[-1]
    t = pl.program_id(2)
    tq = q_ref.shape[1]
    dq = q_ref.shape[2]
    m_rows = r_heads * tq

    def prologue():
        q = q_ref[...].reshape(m_rows, dq)
        sink_tile = None
        if has_sink:
            sink_tile = jnp.concatenate([jnp.broadcast_to(sink_ref[r], (tq, LANES)) for r in range(r_heads)], axis=0)
        return q, sink_tile, _dot_nt(q, kc_ref[0])

    def finish(carry):
        _, l, acc = carry
        o = acc / l
        dv = acc.shape[-1]
        for r in range(r_heads):
            o_ref[:, r * dv:(r + 1) * dv] = o[r * tq:(r + 1) * tq].astype(o_ref.dtype)

    def latent_tile():
        q, sink_tile, s_ctx = prologue()
        if mode == "dense":
            n_chunks = n_lat // tk
            unroll = math.gcd(KV_UNROLL, n_chunks)

            def chunk(c):
                return pl.ds(pl.multiple_of(c * tk, tk), tk)

            def v_ext(v):
                return jnp.concatenate([v, jnp.ones(v.shape, BF16)], axis=1)

            m0 = jnp.max(s_ctx, axis=1, keepdims=True)
            carry = (m0, _dot(jnp.exp(s_ctx - m0).astype(BF16), v_ext(vc_ref[0])))

            def update(carry, s, v):
                m, accl = carry
                m_new = jnp.maximum(m, jnp.max(s, axis=1, keepdims=True))
                p = jnp.exp(s - m_new)
                return m_new, jnp.exp(m - m_new) * accl + _dot(p.astype(BF16), v_ext(v))

            def body(c, carry):
                scores = [_dot_nt(q, kl_ref[0, chunk(unroll * c + u), :]) for u in range(unroll)]
                for u in range(unroll):
                    carry = update(carry, scores[u], vl_ref[0, chunk(unroll * c + u), :])
                return carry

            m_fin, accl = lax.fori_loop(0, n_chunks // unroll, body, carry)
            dv = vc_ref.shape[-1]
            carry = (m_fin, accl[:, dv:], accl[:, :dv])
        else:
            variant = jnp.where(t == 0, 0, jnp.where(t == nt - 1, 2, 1))
            if mode == "window":
                span = tq + 2 * WINDOW
                start = pl.multiple_of(jnp.clip(t * tq - WINDOW, 0, n_lat - span), WINDOW)
                bias = jnp.concatenate([bias_ref[variant]] * r_heads, axis=0)
            else:
                span = NA_KEY_ROWS * GRID_W
                ks_row = jnp.clip(t * (tq // GRID_W) - NA_KH // 2, 0, rows - NA_KEY_ROWS)
                start = pl.multiple_of(ks_row * GRID_W, GRID_W)
                bias = bias_ref[variant].reshape(m_rows, span)
            sl = pl.ds(start, span)
            s_loc = _dot_nt(q, kl_ref[0, sl, :]) + bias
            s_all = jnp.concatenate([s_ctx, s_loc], axis=1)
            v_all = jnp.concatenate([vc_ref[0], vl_ref[0, sl, :]], axis=0)
            carry = _softmax_update(_softmax_empty(m_rows, v_all.shape[1]), s_all, v_all, sink_tile)
        finish(carry)

    def context_tile():
        _, sink_tile, s_ctx = prologue()
        finish(_softmax_first(s_ctx, vc_ref[0], sink_tile))

    if with_ctx:
        pl.when(t < nt)(latent_tile)
        pl.when(t == nt)(context_tile)
    else:
        latent_tile()


def _window_bias():
    span = TQ + 2 * WINDOW
    d = np.arange(TQ)[:, None] - np.arange(span)[None, :]
    offsets = (0, WINDOW, span - TQ)
    return np.stack([np.where(np.abs(d + o) <= WINDOW, 0.0, NEG_INF) for o in offsets]).astype(np.float32)


def _attention(qa, ka, va, *, mode, with_ctx, batch, n_lat, n_ctx, groups, r_heads, q_base, k_base, v_base,
               sink=None, bias=None, tq=TQ, tk=TK):
    dq = qa.shape[2]
    dv = va.shape[2]
    lat_rows = batch * n_lat
    nq = n_lat // tq
    ctx_blk = lat_rows // n_ctx
    tk = min(tk, n_lat)
    assert n_lat % tk == 0 and (not with_ctx or tq == n_ctx)

    def qrow(b, g, t):
        return jnp.where(t < nq, b * nq + t, lat_rows // tq + b) if with_ctx else b * nq + t

    in_specs = [pl.BlockSpec((r_heads, tq, dq), lambda b, g, t: (q_base + g, qrow(b, g, t), 0)),
                pl.BlockSpec((1, n_ctx, dq), lambda b, g, t: (k_base + g, ctx_blk + b, 0)),
                pl.BlockSpec((1, n_ctx, dv), lambda b, g, t: (v_base + g, ctx_blk + b, 0)),
                pl.BlockSpec((1, n_lat, dq), lambda b, g, t: (k_base + g, b, 0)),
                pl.BlockSpec((1, n_lat, dv), lambda b, g, t: (v_base + g, b, 0))]
    args = [qa, ka, va, ka, va]
    if sink is not None:
        in_specs.append(pl.BlockSpec((r_heads, 1, LANES), lambda b, g, t: (g, 0, 0)))
        args.append(jnp.full((sink.shape[0], 1, LANES), NEG_INF, F32).at[:, 0, 0].set(sink))
    if mode == "window":
        in_specs.append(pl.BlockSpec(bias.shape, lambda b, g, t: (0, 0, 0)))
        args.append(bias)
    if mode == "na":
        in_specs.append(pl.BlockSpec((3, r_heads, tq, NA_KEY_ROWS * GRID_W), lambda b, g, t: (0, g, 0, 0)))
        args.append(bias)
    kern = functools.partial(_attn_kernel, mode=mode, r_heads=r_heads, n_lat=n_lat, tk=tk, has_sink=sink is not None,
                             with_ctx=with_ctx, rows=n_lat // GRID_W, nt=nq)
    out_rows = lat_rows + (batch * n_ctx if with_ctx else 0)
    return pl.pallas_call(
        kern,
        out_shape=jax.ShapeDtypeStruct((out_rows, groups * r_heads * dv), BF16),
        grid=(batch, groups, nq + (1 if with_ctx else 0)),
        in_specs=in_specs,
        out_specs=pl.BlockSpec((tq, r_heads * dv), lambda b, g, t: (qrow(b, g, t), g)),
        compiler_params=_cparams(3, 48),
        name="attn_" + mode,
    )(*args)


def _na_variants(rows, q_rows):
    nt = rows // q_rows

    def structure(t):
        ks_row = int(np.clip(t * q_rows - NA_KH // 2, 0, rows - NA_KEY_ROWS))
        out = []
        for qr in range(q_rows):
            r = t * q_rows + qr
            rs = int(np.clip(r - NA_KH // 2, 0, rows - NA_KH))
            out.append([(rs <= ks_row + j < rs + NA_KH, ks_row + j - r + NA_KH - 1) for j in range(NA_KEY_ROWS)])
        return out

    reps = [structure(0), structure(1), structure(nt - 1)]
    for t in range(nt):
        want = reps[0] if t == 0 else reps[2] if t == nt - 1 else reps[1]
        got = structure(t)
        for wq, gq in zip(want, got):
            for (wv, wa), (gv, ga) in zip(wq, gq):
                assert wv == gv and (not wv or wa == ga), "neighbourhood tile variants do not cover this grid"
    return reps


def _na_bias_kernel(rb_ref, o_ref, *, variants):
    c = lax.broadcasted_iota(jnp.int32, (GRID_W, LANES), 0)
    lane = lax.broadcasted_iota(jnp.int32, (GRID_W, LANES), 1)
    kc = lane & (GRID_W - 1)
    cs = jnp.clip(c - NA_KW // 2, 0, GRID_W - NA_KW)
    col_ok = (kc >= cs) & (kc < cs + NA_KW)
    low = lane < GRID_W
    neg = jnp.full((GRID_W, LANES), NEG_INF, F32)
    for v, var in enumerate(variants):
        for qr, krows in enumerate(var):
            for jp in range(NA_KEY_ROWS // 2):
                halves = []
                for hf in range(2):
                    valid, a = krows[2 * jp + hf]
                    if valid:
                        x = jnp.broadcast_to(rb_ref[0, a:a + 1, :], (GRID_W, LANES))
                        shift = (LANES - (NA_KW - 1) + GRID_W * hf) % LANES
                        halves.append(pltpu.roll(x, shift, 1, stride=1, stride_axis=0))
                    else:
                        halves.append(neg)
                tile = jnp.where(col_ok, jnp.where(low, halves[0], halves[1]), NEG_INF)
                o_ref[v, 0, qr * GRID_W:(qr + 1) * GRID_W, jp * LANES:(jp + 1) * LANES] = tile


def _na_bias(rel_bias, rows):
    q_rows = TQ // GRID_W
    variants = _na_variants(rows, q_rows)
    nh, na, nb = rel_bias.shape
    rb = jnp.zeros((nh, 16, LANES), F32).at[:, :na, :nb].set(rel_bias)
    span = NA_KEY_ROWS * GRID_W
    return pl.pallas_call(
        functools.partial(_na_bias_kernel, variants=variants),
        out_shape=jax.ShapeDtypeStruct((3, nh, TQ, span), F32),
        grid=(nh,),
        in_specs=[pl.BlockSpec((1, 16, LANES), lambda h: (h, 0, 0))],
        out_specs=pl.BlockSpec((3, 1, TQ, span), lambda h: (0, h, 0, 0)),
        compiler_params=_cparams(1, 32),
        name="na_bias",
    )(rb)


def _mla_dq_kernel(a_ref, w_ref, g_ref, o_ref):
    o_ref[...] = _rms(_dot(a_ref[...], w_ref[...]), g_ref[...]).astype(BF16)


def _mla_dq(h, w, g, tm=TM):
    t, d = h.shape
    n = w.shape[1]
    return pl.pallas_call(
        _mla_dq_kernel,
        out_shape=jax.ShapeDtypeStruct((t, n), BF16),
        grid=(t // tm,),
        in_specs=[pl.BlockSpec((tm, d), lambda i: (i, 0)),
                  pl.BlockSpec((d, n), lambda i: (0, 0)),
                  pl.BlockSpec((1, n), lambda i: (0, 0))],
        out_specs=pl.BlockSpec((tm, n), lambda i: (i, 0)),
        compiler_params=_cparams(1, 48),
        name="mla_dq",
    )(h, w, g.reshape(1, n))


def _mla_uq_kernel(a_ref, w_ref, cos_ref, s1_ref, s2_ref, o_ref, *, hpt, scale):
    acc = _dot(a_ref[...], w_ref[...])
    for r in range(hpt):
        base = r * MLA_QK
        o_ref[r, :, :MLA_NOPE] = (acc[:, base:base + MLA_NOPE] * scale).astype(BF16)
        pe = _rope64(acc[:, base + MLA_NOPE:base + MLA_QK], cos_ref[...], s1_ref[...], s2_ref[...])
        o_ref[r, :, MLA_NOPE:] = (pe * scale).astype(BF16)


def _mla_uq(cq, w, tabs, *, scale, tm=TM_BIG, tn=TN):
    t, k = cq.shape
    n = w.shape[1]
    hpt = tn // MLA_QK
    cos, s1, s2, tab_idx = tabs
    tab_spec = pl.BlockSpec((tm, LANES), lambda i, j: (tab_idx(i, tm), 0))
    return pl.pallas_call(
        functools.partial(_mla_uq_kernel, hpt=hpt, scale=scale),
        out_shape=jax.ShapeDtypeStruct((n // MLA_QK, t, MLA_QK), BF16),
        grid=(t // tm, n // tn),
        in_specs=[pl.BlockSpec((tm, k), lambda i, j: (i, 0)),
                  pl.BlockSpec((k, tn), lambda i, j: (0, j)),
                  tab_spec, tab_spec, tab_spec],
        out_specs=pl.BlockSpec((hpt, tm, MLA_QK), lambda i, j: (j, i, 0)),
        compiler_params=_cparams(2, 40),
        name="mla_uq",
    )(cq, w, cos, s1, s2)


def _mla_dkv_kernel(a_ref, w_ref, g_ref, cos_ref, s1_ref, s2_ref, ckv_ref, kpe_ref, *, rank):
    acc = _dot(a_ref[...], w_ref[...])
    ckv_ref[...] = _rms(acc[:, :rank], g_ref[...]).astype(BF16)
    kpe_ref[...] = _rope64(acc[:, rank:], cos_ref[...], s1_ref[...], s2_ref[...]).astype(BF16)


def _mla_dkv(h, w, g, tabs, *, rank, tm=TM):
    t, d = h.shape
    n = w.shape[1]
    cos, s1, s2, tab_idx = tabs
    tab_spec = pl.BlockSpec((tm, LANES), lambda i: (tab_idx(i, tm), 0))
    return pl.pallas_call(
        functools.partial(_mla_dkv_kernel, rank=rank),
        out_shape=(jax.ShapeDtypeStruct((t, rank), BF16), jax.ShapeDtypeStruct((t, LANES), BF16)),
        grid=(t // tm,),
        in_specs=[pl.BlockSpec((tm, d), lambda i: (i, 0)),
                  pl.BlockSpec((d, n), lambda i: (0, 0)),
                  pl.BlockSpec((1, rank), lambda i: (0, 0)),
                  tab_spec, tab_spec, tab_spec],
        out_specs=(pl.BlockSpec((tm, rank), lambda i: (i, 0)),
                   pl.BlockSpec((tm, LANES), lambda i: (i, 0))),
        compiler_params=_cparams(1, 40),
        name="mla_dkv",
    )(h, w, g.reshape(1, rank), cos, s1, s2)


def _mla_ukv_kernel(a_ref, w_ref, kpe_ref, k_ref, v_ref, *, hpt):
    acc = _dot(a_ref[...], w_ref[...])
    for r in range(hpt):
        base = r * (MLA_NOPE + MLA_V)
        k_ref[r, :, :MLA_NOPE] = acc[:, base:base + MLA_NOPE].astype(BF16)
        k_ref[r, :, MLA_NOPE:] = kpe_ref[...]
        v_ref[r] = acc[:, base + MLA_NOPE:base + MLA_NOPE + MLA_V].astype(BF16)


def _mla_ukv(ckv, w, kpe, *, tm=TM_BIG, tn=TN):
    t, k = ckv.shape
    n = w.shape[1]
    hpt = tn // (MLA_NOPE + MLA_V)
    nh = n // (MLA_NOPE + MLA_V)
    return pl.pallas_call(
        functools.partial(_mla_ukv_kernel, hpt=hpt),
        out_shape=(jax.ShapeDtypeStruct((nh, t, MLA_QK), BF16), jax.ShapeDtypeStruct((nh, t, MLA_V), BF16)),
        grid=(t // tm, n // tn),
        in_specs=[pl.BlockSpec((tm, k), lambda i, j: (i, 0)),
                  pl.BlockSpec((k, tn), lambda i, j: (0, j)),
                  pl.BlockSpec((tm, LANES), lambda i, j: (i, 0))],
        out_specs=(pl.BlockSpec((hpt, tm, MLA_QK), lambda i, j: (j, i, 0)),
                   pl.BlockSpec((hpt, tm, MLA_V), lambda i, j: (j, i, 0))),
        compiler_params=_cparams(2, 40),
        name="mla_ukv",
    )(ckv, w, kpe)


def _axial_angles(n_tok, rot_dim):
    n_freq = rot_dim // 4
    freqs = ROPE_THETA ** (-jnp.arange(n_freq, dtype=F32) / n_freq)
    t = jnp.arange(n_tok)
    row = (t // GRID_W).astype(F32)
    col = (t % GRID_W).astype(F32)
    return jnp.concatenate([row[:, None] * freqs, col[:, None] * freqs], axis=-1)


def _rope_tables(n_lat, tail):
    ang = _axial_angles(n_lat, HEAD_DIM)
    cos, sin = jnp.cos(ang), jnp.sin(ang)
    cos_t = jnp.concatenate([cos, cos], axis=-1)
    sin_t = jnp.concatenate([-sin, sin], axis=-1)
    cos_t = jnp.concatenate([cos_t, jnp.ones((tail, HEAD_DIM), F32)], axis=0)
    sin_t = jnp.concatenate([sin_t, jnp.zeros((tail, HEAD_DIM), F32)], axis=0)
    return cos_t, sin_t


def _rope64_tables(n_lat, tail):
    ang = _axial_angles(n_lat, MLA_ROPE)
    cos, sin = jnp.cos(ang), jnp.sin(ang)
    half = MLA_ROPE // 2
    z = jnp.zeros((n_lat, half), F32)
    zpad = jnp.zeros((n_lat, LANES - MLA_ROPE), F32)
    cos_t = jnp.concatenate([cos, cos, zpad], axis=-1)
    s1 = jnp.concatenate([z, sin, zpad], axis=-1)
    s2 = jnp.concatenate([-sin, z, zpad], axis=-1)
    ident = jnp.concatenate([jnp.ones((tail, MLA_ROPE), F32), jnp.zeros((tail, LANES - MLA_ROPE), F32)], axis=-1)
    zt = jnp.zeros((tail, LANES), F32)
    return (jnp.concatenate([cos_t, ident], axis=0), jnp.concatenate([s1, zt], axis=0),
            jnp.concatenate([s2, zt], axis=0))


def kernel(x, c, ctx, c_ctx, l0_ada_w, l0_ada_b, l0_attn_norm_g, l0_w_qkv, l0_q_norm_g, l0_k_norm_g, l0_w_o, l0_ffn_norm_g, l0_ffn_w_up, l0_ffn_conv_w, l0_ffn_conv_b, l0_ffn_w_down, l1_ada_w, l1_ada_b, l1_attn_norm_g, l1_w_qkv, l1_sink, l1_w_o, l1_ffn_norm_g, l1_ffn_w_up, l1_ffn_conv_w, l1_ffn_conv_b, l1_ffn_w_down, l2_ada_w, l2_ada_b, l2_attn_norm_g, l2_w_qkv, l2_rel_bias, l2_w_o, l2_ffn_norm_g, l2_ffn_w_up, l2_ffn_conv_w, l2_ffn_conv_b, l2_ffn_w_down, l3_ada_w, l3_ada_b, l3_attn_norm_g, l3_w_dq, l3_q_norm_g, l3_w_uq, l3_w_dkv, l3_kv_norm_g, l3_w_ukv, l3_w_o, l3_ffn_norm_g, l3_ffn_w_up, l3_ffn_conv_w, l3_ffn_conv_b, l3_ffn_w_down, final_norm_g):
    batch, n_lat, d = x.shape
    n_ctx = ctx.shape[1]
    lat_rows = batch * n_lat
    t_rows = lat_rows + batch * n_ctx
    rows = n_lat // GRID_W
    assert batch + 1 <= 8 and n_ctx == TQ and n_lat % TM_BIG == 0 and (batch * n_ctx) % TM_BIG == 0
    assert n_lat & (n_lat - 1) == 0 and rows >= NA_KEY_ROWS and n_lat >= TQ + 2 * WINDOW

    def seg(i, tile):
        return jnp.minimum(i * tile // n_lat, batch)

    def tab_idx(i, tile):
        return jnp.where(i < lat_rows // tile, i % (n_lat // tile), n_lat // tile)

    cos_t, sin_t = _rope_tables(n_lat, TM_BIG)
    tabs128 = (cos_t, sin_t, tab_idx)
    tabs64 = _rope64_tables(n_lat, TM_BIG) + (tab_idx,)
    ones_g = jnp.ones((HEAD_DIM,), F32)

    xa = (x.reshape(lat_rows, d), ctx.reshape(batch * n_ctx, d))
    c8 = jnp.zeros((8, d), F32).at[:batch].set(c).at[batch].set(c_ctx)

    ada = [(l0_ada_w, l0_ada_b), (l1_ada_w, l1_ada_b), (l2_ada_w, l2_ada_b), (l3_ada_w, l3_ada_b)]
    attn_g = [l0_attn_norm_g, l1_attn_norm_g, l2_attn_norm_g, l3_attn_norm_g]
    ffn_g = [l0_ffn_norm_g, l1_ffn_norm_g, l2_ffn_norm_g, l3_ffn_norm_g]
    w_o = [l0_w_o, l1_w_o, l2_w_o, l3_w_o]
    ffn_p = [(l0_ffn_w_up, l0_ffn_conv_w, l0_ffn_conv_b, l0_ffn_w_down),
             (l1_ffn_w_up, l1_ffn_conv_w, l1_ffn_conv_b, l1_ffn_w_down),
             (l2_ffn_w_up, l2_ffn_conv_w, l2_ffn_conv_b, l2_ffn_w_down),
             (l3_ffn_w_up, l3_ffn_conv_w, l3_ffn_conv_b, l3_ffn_w_down)]
    w_qkv = [l0_w_qkv, l1_w_qkv, l2_w_qkv]
    gqa_scale = HEAD_DIM ** -0.5
    common = dict(batch=batch, n_lat=n_lat, n_ctx=n_ctx)
    gqa = dict(groups=N_KV_HEADS, r_heads=Q_PER_KV, q_base=0, k_base=N_HEADS, v_base=N_HEADS + N_KV_HEADS,
               **common)
    depth = 4

    for layer in range(depth):
        need_ctx = layer < depth - 1
        mod = _ada(c8, *ada[layer]).reshape(8, 6, d)
        h = _normmod(xa, attn_g[layer], mod, seg, rows=t_rows, shift_idx=0, scale_idx=1)

        if layer < 3:
            w = w_qkv[layer]
            if layer == 0:
                qkv = _qkv_proj(h, w, tabs128, l0_q_norm_g, l0_k_norm_g, use_norm=True, use_rope=True,
                                scale=gqa_scale)
            elif layer == 1:
                qkv = _qkv_proj(h, w, tabs128, ones_g, ones_g, use_norm=False, use_rope=True, scale=gqa_scale)
            else:
                qkv = _qkv_proj(h, w, tabs128, ones_g, ones_g, use_norm=False, use_rope=False, scale=gqa_scale)
            if layer == 0:
                o = _attention(qkv, qkv, qkv, mode="dense", with_ctx=True, **gqa)
            elif layer == 1:
                o = _attention(qkv, qkv, qkv, mode="window", with_ctx=True, sink=l1_sink,
                               bias=jnp.asarray(_window_bias()), **gqa)
            else:
                o = _attention(qkv, qkv, qkv, mode="na", with_ctx=True, bias=_na_bias(l2_rel_bias, rows), **gqa)
        else:
            mla_scale = (MLA_NOPE + MLA_ROPE) ** -0.5
            rq = l3_w_uq.shape[0]
            w_uq = l3_w_uq.reshape(rq, N_HEADS, MLA_NOPE + MLA_ROPE)
            w_uq = jnp.pad(w_uq, ((0, 0), (0, 0), (0, MLA_QK - MLA_NOPE - MLA_ROPE))).reshape(rq, N_HEADS * MLA_QK)
            rank = l3_kv_norm_g.shape[0]
            w_dkv = jnp.pad(l3_w_dkv, ((0, 0), (0, LANES - MLA_ROPE)))
            cq = _mla_dq(h, l3_w_dq.astype(BF16), l3_q_norm_g)
            qh = _mla_uq(cq, w_uq.astype(BF16), tabs64, scale=mla_scale)
            ckv, kpe = _mla_dkv(h, w_dkv.astype(BF16), l3_kv_norm_g, tabs64, rank=rank)
            kh, vh = _mla_ukv(ckv, l3_w_ukv.astype(BF16), kpe)
            o = _attention(qh, kh, vh, mode="dense", with_ctx=False, groups=N_HEADS, r_heads=1, q_base=0, k_base=0,
                           v_base=0, tq=4 * TQ, tk=2 * TK, **common)

        out_rows = t_rows if need_ctx else lat_rows
        xa = _proj_res(o, w_o[layer], xa, mod, seg, rows=out_rows, gate_idx=2)
        h2 = _normmod(xa, ffn_g[layer], mod, seg, rows=out_rows, shift_idx=3, scale_idx=4)
        w_up, conv_w, conv_b, w_down = ffn_p[layer]
        u = _proj_plain(h2, w_up, rows=out_rows)
        xa = _ffn_down(u, conv_w, conv_b, w_down.astype(BF16), xa, mod, seg, rows=out_rows, n_lat=n_lat,
                       n_ctx=n_ctx, lat_rows=lat_rows, gate_idx=5)

    return _final_norm(xa, final_norm_g, rows=lat_rows).reshape(batch, n_lat, d)
```
